```python
import jax, jax.numpy as jnp
from jax import lax
import numpy as np

D_MODEL = 1024
BATCH = 8
SEQ = 4096
DEPTH = 1

EPS = 1e-6
D_FF = 2816
CONV_W = 4
GDN_HEADS = 8
GDN_DK = 128
GDN_DV = 128
GDN_CHUNK = 64
MLA_HEADS = 8
Q_LORA = 384
KV_LORA = 256
QK_NOPE = 128
QK_ROPE = 64
V_HEAD = 128
ROPE_THETA = 10000.0
Q_BLOCK = 128

GDN_QK_W = GDN_HEADS * GDN_DK
GDN_V_W = GDN_HEADS * GDN_DV
MLA_V_W = MLA_HEADS * V_HEAD
IN_WIDTHS = (GDN_QK_W, GDN_QK_W, GDN_V_W, GDN_V_W, GDN_HEADS, GDN_HEADS,
             Q_LORA, KV_LORA + QK_ROPE, D_MODEL, D_MODEL)
D_IN = 2 * GDN_QK_W + 2 * GDN_V_W + 2 * GDN_HEADS + Q_LORA + KV_LORA + QK_ROPE + 2 * D_MODEL
CONV_CH = 2 * GDN_QK_W + GDN_V_W

kernel_name = "hybrid_gdn_mla_macaron"


def rms_norm(x, w):
    xf = x.astype(jnp.float32)
    xf = xf * lax.rsqrt(jnp.mean(xf * xf, axis=-1, keepdims=True) + EPS)
    return (xf * w.astype(jnp.float32)).astype(x.dtype)


def l2_norm(x):
    return x * lax.rsqrt(jnp.sum(x * x, axis=-1, keepdims=True) + EPS)


def swiglu(x, w_gate, w_up, w_down):
    return (jax.nn.silu(x @ w_gate) * (x @ w_up)) @ w_down


def causal_depthwise_conv(x, w):
    c = x.shape[-1]
    return lax.conv_general_dilated(
        x, w[:, None, :].astype(x.dtype), window_strides=(1,), padding=[(CONV_W - 1, 0)],
        dimension_numbers=('NWC', 'WIO', 'NWC'), feature_group_count=c)


def gated_delta_rule_chunked(q, k, v, g, beta):
    b, t, h, dk = q.shape
    dv = v.shape[-1]
    c = GDN_CHUNK
    n = t // c

    def chunks(a):
        return jnp.moveaxis(a.reshape((b, n, c, h) + a.shape[3:]), 3, 1)

    q, k, v, g, beta = chunks(q), chunks(k), chunks(v), chunks(g), chunks(beta)
    gc = jnp.cumsum(g, axis=-1)
    idx = jnp.arange(c)
    causal = idx[:, None] >= idx[None, :]
    strict = idx[:, None] > idx[None, :]
    decay = jnp.exp(jnp.where(causal, gc[..., :, None] - gc[..., None, :], -jnp.inf))
    kb = k * beta[..., None]
    vb = v * beta[..., None]
    lmat = jnp.where(strict, jnp.einsum('bhncd,bhnsd->bhncs', kb, k) * decay, 0.0)
    rhs = jnp.concatenate([vb, kb * jnp.exp(gc)[..., None]], axis=-1)
    sol = lax.linalg.triangular_solve(lmat, rhs, left_side=True, lower=True, unit_diagonal=True)
    u_c, w_c = sol[..., :dv], sol[..., dv:]
    attn = jnp.einsum('bhncd,bhnsd->bhncs', q, k) * decay
    q_dec = q * jnp.exp(gc)[..., None]
    k_dec = k * jnp.exp(gc[..., -1:] - gc)[..., None]
    g_tot = jnp.exp(gc[..., -1])
    xs = tuple(jnp.moveaxis(a, 2, 0) for a in (attn, u_c, w_c, q_dec, k_dec, g_tot))

    def step(state, inp):
        attn_i, u_i, w_i, qd_i, kd_i, gt_i = inp
        v_new = u_i - jnp.einsum('bhcd,bhde->bhce', w_i, state)
        o_i = jnp.einsum('bhcd,bhde->bhce', qd_i, state) + jnp.einsum('bhcs,bhse->bhce', attn_i, v_new)
        state = state * gt_i[..., None, None] + jnp.einsum('bhcd,bhce->bhde', kd_i, v_new)
        return state, o_i

    s0 = jnp.zeros((b, h, dk, dv), jnp.float32)
    _, o = lax.scan(step, s0, xs)
    o = jnp.moveaxis(o, 0, 2)
    return jnp.moveaxis(o, 1, 3).reshape(b, t, h, dv)


def gdn_branch(qa, ka, va, za, ba, aa, conv_w, a_log, dt_bias, gdn_norm, proj_a):
    b, t, _ = qa.shape
    qkv = jax.nn.silu(causal_depthwise_conv(jnp.concatenate([qa, ka, va], axis=-1), conv_w))
    q, k, v = jnp.split(qkv, [GDN_QK_W, 2 * GDN_QK_W], axis=-1)
    q = l2_norm(q.reshape(b, t, GDN_HEADS, GDN_DK).astype(jnp.float32)) * (GDN_DK ** -0.5)
    k = l2_norm(k.reshape(b, t, GDN_HEADS, GDN_DK).astype(jnp.float32))
    v = v.reshape(b, t, GDN_HEADS, GDN_DV).astype(jnp.float32)
    beta = jax.nn.sigmoid(ba.astype(jnp.float32))
    g = -jnp.exp(a_log.astype(jnp.float32)) * jax.nn.softplus(
        aa.astype(jnp.float32) + dt_bias.astype(jnp.float32))
    o = gated_delta_rule_chunked(q, k, v, g, beta)
    z = za.reshape(b, t, GDN_HEADS, GDN_DV).astype(jnp.float32)
    o = rms_norm(o, gdn_norm) * jax.nn.silu(z)
    return o.reshape(b, t, GDN_V_W).astype(qa.dtype) @ proj_a


def rope_cos_sin(positions):
    inv_freq = ROPE_THETA ** (-jnp.arange(0, QK_ROPE, 2, dtype=jnp.float32) / QK_ROPE)
    ang = positions.astype(jnp.float32)[..., None] * inv_freq
    return jnp.cos(ang), jnp.sin(ang)


def apply_rope(x, cos, sin):
    half = QK_ROPE // 2
    x1, x2 = x[..., :half], x[..., half:]
    cos = cos.astype(x.dtype)
    sin = sin.astype(x.dtype)
    return jnp.concatenate([x1 * cos - x2 * sin, x2 * cos + x1 * sin], axis=-1)


def causal_block_attention(q_nope, q_pe, k_nope, k_pe, v):
    b, t, h, _ = q_nope.shape
    nb = t // Q_BLOCK
    qn = jnp.moveaxis(q_nope.reshape(b, nb, Q_BLOCK, h, QK_NOPE), 1, 0)
    qp = jnp.moveaxis(q_pe.reshape(b, nb, Q_BLOCK, h, QK_ROPE), 1, 0)
    key_idx = jnp.arange(t)
    scale = (QK_NOPE + QK_ROPE) ** -0.5

    def block(args):
        qn_b, qp_b, blk = args
        s = (jnp.einsum('bqhd,bkhd->bhqk', qn_b, k_nope)
             + jnp.einsum('bqhr,bkr->bhqk', qp_b, k_pe)).astype(jnp.float32) * scale
        q_idx = blk * Q_BLOCK + jnp.arange(Q_BLOCK)
        s = jnp.where(key_idx[None, :] <= q_idx[:, None], s, -jnp.inf)
        p = jax.nn.softmax(s, axis=-1).astype(v.dtype)
        return jnp.einsum('bhqk,bkhd->bqhd', p, v)

    o = lax.map(block, (qn, qp, jnp.arange(nb)))
    return jnp.moveaxis(o, 0, 1).reshape(b, t, h, V_HEAD)


def mla_branch(qd, kvd, positions, q_a_norm, w_q_up, kv_a_norm, w_kv_up, proj_b):
    b, t, _ = qd.shape
    q = (rms_norm(qd, q_a_norm) @ w_q_up).reshape(b, t, MLA_HEADS, QK_NOPE + QK_ROPE)
    q_nope, q_pe = q[..., :QK_NOPE], q[..., QK_NOPE:]
    c_kv, k_pe = kvd[..., :KV_LORA], kvd[..., KV_LORA:]
    kv = (rms_norm(c_kv, kv_a_norm) @ w_kv_up).reshape(b, t, MLA_HEADS, QK_NOPE + V_HEAD)
    k_nope, v = kv[..., :QK_NOPE], kv[..., QK_NOPE:]
    cos, sin = rope_cos_sin(positions)
    q_pe = apply_rope(q_pe, cos[:, :, None, :], sin[:, :, None, :])
    k_pe = apply_rope(k_pe, cos, sin)
    o = causal_block_attention(q_nope, q_pe, k_nope, k_pe, v)
    return o.reshape(b, t, MLA_V_W) @ proj_b


def hybrid_mixer(u, positions, w_in, conv_w, a_log, dt_bias, gdn_norm, proj_a,
                 q_a_norm, w_q_up, kv_a_norm, w_kv_up, proj_b, w_o):
    proj = u @ w_in
    offsets = [int(o) for o in np.cumsum(IN_WIDTHS)[:-1]]
    qa, ka, va, za, ba, aa, qd, kvd, gate_a, gate_b = jnp.split(proj, offsets, axis=-1)
    y_a = gdn_branch(qa, ka, va, za, ba, aa, conv_w, a_log, dt_bias, gdn_norm, proj_a)
    y_b = mla_branch(qd, kvd, positions, q_a_norm, w_q_up, kv_a_norm, w_kv_up, proj_b)
    merged = jax.nn.sigmoid(gate_a) * y_a + jax.nn.sigmoid(gate_b) * y_b
    return merged @ w_o


def setup_inputs(seed: int = 0) -> dict:
    key = jax.random.key(seed)
    ks = jax.random.split(key, 26)

    def dense(k, fan_in, fan_out):
        return jax.random.normal(k, (DEPTH, fan_in, fan_out), jnp.float32) * fan_in ** -0.5

    def gain(k, n):
        return 1.0 + 0.02 * jax.random.normal(k, (DEPTH, n), jnp.float32)

    x = jax.random.normal(ks[0], (BATCH, SEQ, D_MODEL), jnp.float32)
    positions = (jax.random.randint(ks[1], (BATCH, 1), 0, 4096, dtype=jnp.int32)
                 + jnp.arange(SEQ, dtype=jnp.int32)[None, :])
    a_log = jnp.log(jax.random.uniform(ks[9], (DEPTH, GDN_HEADS), jnp.float32, 1.0, 16.0))
    dt = jnp.exp(jax.random.uniform(ks[10], (DEPTH, GDN_HEADS), jnp.float32,
                                    np.log(0.001).astype(np.float32), np.log(0.1).astype(np.float32)))
    dt_bias = dt + jnp.log(-jnp.expm1(-dt))
    return {
        "x": x,
        "positions": positions,
        "ffn1_norm": gain(ks[2], D_MODEL),
        "ffn1_w_gate": dense(ks[3], D_MODEL, D_FF),
        "ffn1_w_up": dense(ks[4], D_MODEL, D_FF),
        "ffn1_w_down": dense(ks[5], D_FF, D_MODEL),
        "mix_norm": gain(ks[6], D_MODEL),
        "w_in": dense(ks[7], D_MODEL, D_IN),
        "conv_w": jax.random.normal(ks[8], (DEPTH, CONV_W, CONV_CH), jnp.float32) * CONV_W ** -0.5,
        "a_log": a_log,
        "dt_bias": dt_bias,
        "gdn_norm": gain(ks[11], GDN_DV),
        "proj_a": dense(ks[12], GDN_V_W, D_MODEL),
        "q_a_norm": gain(ks[13], Q_LORA),
        "w_q_up": dense(ks[14], Q_LORA, MLA_HEADS * (QK_NOPE + QK_ROPE)),
        "kv_a_norm": gain(ks[15], KV_LORA),
        "w_kv_up": dense(ks[16], KV_LORA, MLA_HEADS * (QK_NOPE + V_HEAD)),
        "proj_b": dense(ks[17], MLA_V_W, D_MODEL),
        "w_o": dense(ks[18], D_MODEL, D_MODEL),
        "ffn2_norm": gain(ks[19], D_MODEL),
        "ffn2_w_gate": dense(ks[20], D_MODEL, D_FF),
        "ffn2_w_up": dense(ks[21], D_MODEL, D_FF),
        "ffn2_w_down": dense(ks[22], D_FF, D_MODEL),
        "final_norm": 1.0 + 0.02 * jax.random.normal(ks[23], (D_MODEL,), jnp.float32),
    }


def reference(x, positions, ffn1_norm, ffn1_w_gate, ffn1_w_up, ffn1_w_down, mix_norm, w_in,
              conv_w, a_log, dt_bias, gdn_norm, proj_a, q_a_norm, w_q_up, kv_a_norm, w_kv_up,
              proj_b, w_o, ffn2_norm, ffn2_w_gate, ffn2_w_up, ffn2_w_down, final_norm):
    h = x
    for l in range(DEPTH):
        h = h + 0.5 * swiglu(rms_norm(h, ffn1_norm[l]), ffn1_w_gate[l], ffn1_w_up[l], ffn1_w_down[l])
        u = rms_norm(h, mix_norm[l])
        h = h + hybrid_mixer(u, positions, w_in[l], conv_w[l], a_log[l], dt_bias[l], gdn_norm[l],
                             proj_a[l], q_a_norm[l], w_q_up[l], kv_a_norm[l], w_kv_up[l],
                             proj_b[l], w_o[l])
        h = h + 0.5 * swiglu(rms_norm(h, ffn2_norm[l]), ffn2_w_gate[l], ffn2_w_up[l], ffn2_w_down[l])
    return rms_norm(h, final_norm)
```

```python
import functools

import numpy as np
import jax
import jax.numpy as jnp
from jax import lax
from jax.experimental import pallas as pl
from jax.experimental.pallas import tpu as pltpu

EPS = 1e-6
CONV_W = 4
GDN_HEADS = 8
GDN_DK = 128
GDN_DV = 128
GDN_CHUNK = 64
MLA_HEADS = 8
Q_LORA = 384
KV_LORA = 256
QK_NOPE = 128
QK_ROPE = 64
V_HEAD = 128
ROPE_THETA = 10000.0
LANES = 128
CONV_HALO = 8

BF = jnp.bfloat16
F32 = jnp.float32
NEG_BIG = -1e30


def _dot(a, b):
    return jnp.dot(a, b, preferred_element_type=F32)


def _dot_nt(a, b):
    return lax.dot_general(a, b, (((1,), (1,)), ((), ())), preferred_element_type=F32)


def _dot_tn(a, b):
    return lax.dot_general(a, b, (((0,), (0,)), ((), ())), preferred_element_type=F32)


def _rms(x, w):
    return x * lax.rsqrt(jnp.mean(x * x, axis=-1, keepdims=True) + EPS) * w


def _sigmoid(x):
    return 1.0 / (1.0 + jnp.exp(-x))


def _resident(shape):
    nd = len(shape)
    return pl.BlockSpec(shape, lambda *_: (0,) * nd, pipeline_mode=pl.Buffered(1))


def _ffn_kernel(x_ref, nw_ref, wg_ref, wu_ref, wd_ref, fw_ref, o_ref, *, final):
    x = x_ref[...]
    xn = _rms(x, nw_ref[...]).astype(BF)
    g = _dot(xn, wg_ref[...])
    u = _dot(xn, wu_ref[...])
    a = (g * _sigmoid(g) * u).astype(BF)
    h = x + 0.5 * _dot(a, wd_ref[...])
    if final:
        h = _rms(h, fw_ref[...])
    o_ref[...] = h


def _ffn(x, nw, wg, wu, wd, fw, *, final, tm):
    n, d = x.shape
    ff = wg.shape[1]
    row = pl.BlockSpec((tm, d), lambda i: (i, 0))
    return pl.pallas_call(
        functools.partial(_ffn_kernel, final=final),
        grid=(n // tm,),
        in_specs=[row, _resident((1, d)), _resident((d, ff)), _resident((d, ff)),
                  _resident((ff, d)), _resident((1, d))],
        out_specs=row,
        out_shape=jax.ShapeDtypeStruct((n, d), F32),
        compiler_params=pltpu.CompilerParams(
            dimension_semantics=("arbitrary",), vmem_limit_bytes=56 * 1024 * 1024),
        name="ffn_final" if final else "ffn",
    )(x, nw, wg, wu, wd, fw)


def _inproj_kernel(pos_ref, h_ref, nw_ref, wqkv_ref, convw_ref, wz_ref, wbg_ref, gp_ref,
                   wqd_ref, wckv_ref, wkpe_ref, wga_ref, wgb_ref, qan_ref, wqn_ref, wqp_ref,
                   wqpr_ref, kvan_ref, wkn_ref, wv_ref, invf_ref,
                   qg_ref, kg_ref, vg_ref, z_ref, bg_ref, qm_ref, km_ref, vm_ref, sa_ref, sb_ref,
                   xbuf_ref, *, tm):
    u = _rms(h_ref[0], nw_ref[...]).astype(BF)

    @pl.when(pl.program_id(1) == 0)
    def _():
        xbuf_ref[0:CONV_HALO, :] = jnp.zeros((CONV_HALO, xbuf_ref.shape[1]), F32)

    xbuf_ref[CONV_HALO:CONV_HALO + tm, :] = _dot(u, wqkv_ref[...])
    n_slab = xbuf_ref.shape[1] // LANES
    per = n_slab // 3
    for s in range(n_slab):
        cols = slice(LANES * s, LANES * (s + 1))
        base = CONV_HALO - (CONV_W - 1)
        acc = convw_ref[0:1, cols] * xbuf_ref[base:base + tm, cols]
        for j in range(1, CONV_W):
            acc = acc + convw_ref[j:j + 1, cols] * xbuf_ref[base + j:base + j + tm, cols]
        y = acc * _sigmoid(acc)
        if s < 2 * per:
            y = y * lax.rsqrt(jnp.sum(y * y, axis=-1, keepdims=True) + EPS)
        if s < per:
            y = y * (GDN_DK ** -0.5)
        dst = (qg_ref, kg_ref, vg_ref)[s // per]
        oc = slice(LANES * (s % per), LANES * (s % per + 1))
        dst[0, :, oc] = y.astype(BF)
    xbuf_ref[0:CONV_HALO, :] = xbuf_ref[tm:tm + CONV_HALO, :]

    zz = _dot(u, wz_ref[...])
    z_ref[0] = (zz * _sigmoid(zz)).astype(BF)

    ba = _dot(u, wbg_ref[...])
    sp_in = ba + gp_ref[1:2, :]
    softplus = jnp.maximum(sp_in, 0.0) + jnp.log1p(jnp.exp(-jnp.abs(sp_in)))
    lane = lax.broadcasted_iota(jnp.int32, ba.shape, 1)
    bg_ref[0] = jnp.where(lane < GDN_HEADS, _sigmoid(ba), -jnp.exp(gp_ref[0:1, :]) * softplus)

    sa_ref[0] = _sigmoid(_dot(u, wga_ref[...])).astype(BF)
    sb_ref[0] = _sigmoid(_dot(u, wgb_ref[...])).astype(BF)

    qn = _rms(_dot(u, wqd_ref[...]), qan_ref[...]).astype(BF)
    cn = _rms(_dot(u, wckv_ref[...]), kvan_ref[...]).astype(BF)
    ang = pos_ref[0].astype(F32) * invf_ref[...]
    cos = jnp.cos(ang)
    sin = jnp.sin(ang)
    reps = (MLA_HEADS * QK_ROPE) // LANES
    cos_h = jnp.concatenate([cos] * reps, axis=-1)
    sin_h = jnp.concatenate([sin] * reps, axis=-1)
    scale = (QK_NOPE + QK_ROPE) ** -0.5
    q_nope = _dot(qn, wqn_ref[...]) * scale
    q_pe = (_dot(qn, wqp_ref[...]) * cos_h + _dot(qn, wqpr_ref[...]) * sin_h) * scale
    kp = _dot(u, wkpe_ref[...])
    k_pe = (kp[:, :QK_ROPE] * cos[:, :QK_ROPE] + kp[:, QK_ROPE:] * sin[:, :QK_ROPE]).astype(BF)
    k_nope = _dot(cn, wkn_ref[...])
    v = _dot(cn, wv_ref[...])
    for hh in range(MLA_HEADS):
        qm_ref[0, hh, :, 0:QK_NOPE] = q_nope[:, QK_NOPE * hh:QK_NOPE * (hh + 1)].astype(BF)
        qm_ref[0, hh, :, QK_NOPE:] = q_pe[:, QK_ROPE * hh:QK_ROPE * (hh + 1)].astype(BF)
        km_ref[0, hh, :, 0:QK_NOPE] = k_nope[:, QK_NOPE * hh:QK_NOPE * (hh + 1)].astype(BF)
        km_ref[0, hh, :, QK_NOPE:] = k_pe
        vm_ref[0, hh] = v[:, V_HEAD * hh:V_HEAD * (hh + 1)].astype(BF)


def _inproj(pos, h, weights, *, tm):
    b, t, d = h.shape
    nt = t // tm
    conv_ch = weights[1].shape[1]
    qk_w = GDN_HEADS * GDN_DK
    v_w = GDN_HEADS * GDN_DV
    hq = QK_NOPE + QK_ROPE

    def row(width):
        return pl.BlockSpec((1, tm, width), lambda bi, ti: (bi, ti, 0))

    def head(width):
        return pl.BlockSpec((1, MLA_HEADS, tm, width), lambda bi, ti: (bi, 0, ti, 0))

    out_shape = (
        jax.ShapeDtypeStruct((b, t, qk_w), BF), jax.ShapeDtypeStruct((b, t, qk_w), BF),
        jax.ShapeDtypeStruct((b, t, v_w), BF), jax.ShapeDtypeStruct((b, t, v_w), BF),
        jax.ShapeDtypeStruct((b, t, 2 * GDN_HEADS), F32),
        jax.ShapeDtypeStruct((b, MLA_HEADS, t, hq), BF), jax.ShapeDtypeStruct((b, MLA_HEADS, t, hq), BF),
        jax.ShapeDtypeStruct((b, MLA_HEADS, t, V_HEAD), BF),
        jax.ShapeDtypeStruct((b, t, d), BF), jax.ShapeDtypeStruct((b, t, d), BF),
    )
    out_specs = (row(qk_w), row(qk_w), row(v_w), row(v_w), row(2 * GDN_HEADS),
                 head(hq), head(hq), head(V_HEAD), row(d), row(d))
    return pl.pallas_call(
        functools.partial(_inproj_kernel, tm=tm),
        grid=(b, nt),
        in_specs=[row(1), row(d)] + [_resident(w.shape) for w in weights],
        out_specs=out_specs,
        out_shape=out_shape,
        scratch_shapes=[pltpu.VMEM((tm + CONV_HALO, conv_ch), F32)],
        compiler_params=pltpu.CompilerParams(
            dimension_semantics=("arbitrary", "arbitrary"), vmem_limit_bytes=56 * 1024 * 1024),
        name="inproj",
    )(pos, h, *weights)


def _gdn_kernel(q_ref, k_ref, v_ref, z_ref, bg_ref, nw_ref, o_ref, state_ref, *, tc):
    c = GDN_CHUNK

    @pl.when(pl.program_id(1) == 0)
    def _():
        state_ref[...] = jnp.zeros(state_ref.shape, F32)

    ii = lax.broadcasted_iota(jnp.int32, (c, c), 0)
    jj = lax.broadcasted_iota(jnp.int32, (c, c), 1)
    eye = (ii == jj).astype(F32)

    def chunk_body(ci, carry):
        r0 = pl.multiple_of(ci * c, c)
        rows = pl.ds(r0, c)
        bg = bg_ref[0, rows, :]
        for hh in range(GDN_HEADS):
            cols = slice(LANES * hh, LANES * (hh + 1))
            q = q_ref[0, rows, cols]
            k = k_ref[0, rows, cols]
            v = v_ref[0, rows, cols]
            beta = bg[:, hh:hh + 1]
            g = bg[:, GDN_HEADS + hh:GDN_HEADS + hh + 1]
            gc_row = jnp.sum(jnp.where(ii <= jj, g, 0.0), axis=0, keepdims=True)
            gc_col = jnp.sum(eye * gc_row, axis=1, keepdims=True)
            g_last = jnp.sum(g, axis=0, keepdims=True)
            decay = jnp.where(ii >= jj, jnp.exp(jnp.minimum(gc_col - gc_row, 0.0)), 0.0)
            e_col = jnp.exp(gc_col)
            kf = k.astype(F32)
            kb = kf * beta
            both = _dot_nt(jnp.concatenate([kb.astype(BF), q], axis=0), k)
            n = jnp.where(ii > jj, -both[:c] * decay, 0.0)
            attn = both[c:] * decay
            a = eye + n
            nb = n.astype(BF)
            p = _dot(nb, nb)
            for _ in range(4):
                pb = p.astype(BF)
                sq = _dot(jnp.concatenate([pb, a.astype(BF)], axis=0), pb)
                p = sq[:c]
                a = a + sq[c:]
            a = a + _dot(a.astype(BF), p.astype(BF))
            rhs = jnp.concatenate([v.astype(F32) * beta, kb * e_col], axis=1).astype(BF)
            sol = _dot(a.astype(BF), rhs)
            u_c = sol[:, :GDN_DV]
            w_c = sol[:, GDN_DV:]
            s_old = state_ref[hh]
            q_dec = (q.astype(F32) * e_col).astype(BF)
            ws = _dot(jnp.concatenate([w_c.astype(BF), q_dec], axis=0), s_old.astype(BF))
            v_new = (u_c - ws[:c]).astype(BF)
            o = ws[c:] + _dot(attn.astype(BF), v_new)
            k_dec = (kf * jnp.exp(g_last - gc_col)).astype(BF)
            state_ref[hh] = s_old * jnp.exp(g_last) + _dot_tn(k_dec, v_new)
            on = _rms(o, nw_ref[...])
            o_ref[0, rows, cols] = (on * z_ref[0, rows, cols].astype(F32)).astype(BF)
        return carry

    lax.fori_loop(0, tc // c, chunk_body, 0)


def _gdn(qg, kg, vg, z, bg, nw, *, tc):
    b, t, w = qg.shape
    row = pl.BlockSpec((1, tc, w), lambda bi, ti: (bi, ti, 0))
    return pl.pallas_call(
        functools.partial(_gdn_kernel, tc=tc),
        grid=(b, t // tc),
        in_specs=[row, row, row, row,
                  pl.BlockSpec((1, tc, bg.shape[2]), lambda bi, ti: (bi, ti, 0)),
                  _resident(nw.shape)],
        out_specs=row,
        out_shape=jax.ShapeDtypeStruct((b, t, w), BF),
        scratch_shapes=[pltpu.VMEM((GDN_HEADS, GDN_DK, GDN_DV), F32)],
        compiler_params=pltpu.CompilerParams(dimension_semantics=("arbitrary", "arbitrary")),
        name="gdn",
    )(qg, kg, vg, z, bg, nw)


def _attn_kernel(q_ref, k_ref, v_ref, o_ref, m_ref, l_ref, acc_ref, *, tq):
    t = q_ref.shape[2]
    ii = lax.broadcasted_iota(jnp.int32, (tq, tq), 0)
    jj = lax.broadcasted_iota(jnp.int32, (tq, tq), 1)

    def step(q, k0, masked):
        k = k_ref[0, 0, pl.ds(k0, tq), :]
        v = v_ref[0, 0, pl.ds(k0, tq), :]
        s = _dot_nt(q, k)
        if masked:
            s = jnp.where(jj <= ii, s, NEG_BIG)
        m_old = m_ref[...]
        m_new = jnp.maximum(m_old, jnp.max(s, axis=-1, keepdims=True))
        p = jnp.exp(s - m_new)
        alpha = jnp.exp(m_old - m_new)
        l_ref[...] = alpha * l_ref[...] + jnp.sum(p, axis=-1, keepdims=True)
        acc_ref[...] = alpha * acc_ref[...] + _dot(p.astype(BF), v)
        m_ref[...] = m_new

    def q_body(qi, carry):
        q0 = pl.multiple_of(qi * tq, tq)
        q = q_ref[0, 0, pl.ds(q0, tq), :]
        m_ref[...] = jnp.full(m_ref.shape, NEG_BIG, F32)
        l_ref[...] = jnp.zeros(l_ref.shape, F32)
        acc_ref[...] = jnp.zeros(acc_ref.shape, F32)

        def kv_body(ki, c2):
            step(q, pl.multiple_of(ki * tq, tq), False)
            return c2

        lax.fori_loop(0, qi, kv_body, 0)
        step(q, q0, True)
        o_ref[0, pl.ds(q0, tq), :] = (acc_ref[...] / l_ref[...]).astype(BF)
        return carry

    lax.fori_loop(0, t // tq, q_body, 0)


def _attn(qm, km, vm, *, tq):
    b, h, t, hq = qm.shape
    dv = vm.shape[3]
    return pl.pallas_call(
        functools.partial(_attn_kernel, tq=tq),
        grid=(b, h),
        in_specs=[pl.BlockSpec((1, 1, t, hq), lambda bi, hi: (bi, hi, 0, 0)),
                  pl.BlockSpec((1, 1, t, hq), lambda bi, hi: (bi, hi, 0, 0)),
                  pl.BlockSpec((1, 1, t, dv), lambda bi, hi: (bi, hi, 0, 0))],
        out_specs=pl.BlockSpec((1, t, dv), lambda bi, hi: (bi, 0, hi)),
        out_shape=jax.ShapeDtypeStruct((b, t, h * dv), BF),
        scratch_shapes=[pltpu.VMEM((tq, 1), F32), pltpu.VMEM((tq, 1), F32),
                        pltpu.VMEM((tq, dv), F32)],
        compiler_params=pltpu.CompilerParams(dimension_semantics=("arbitrary", "arbitrary")),
        name="attn",
    )(qm, km, vm)


def _merge_kernel(h_ref, oa_ref, ob_ref, sa_ref, sb_ref, pa_ref, pb_ref, wo_ref, o_ref):
    ya = _dot(oa_ref[...], pa_ref[...])
    yb = _dot(ob_ref[...], pb_ref[...])
    merged = sa_ref[...].astype(F32) * ya + sb_ref[...].astype(F32) * yb
    o_ref[...] = h_ref[...] + _dot(merged.astype(BF), wo_ref[...])


def _merge(h, oa, ob, sa, sb, pa, pb, wo, *, tm):
    n, d = h.shape
    row = pl.BlockSpec((tm, d), lambda i: (i, 0))
    return pl.pallas_call(
        _merge_kernel,
        grid=(n // tm,),
        in_specs=[row, row, row, row, row, _resident(pa.shape), _resident(pb.shape),
                  _resident(wo.shape)],
        out_specs=row,
        out_shape=jax.ShapeDtypeStruct((n, d), F32),
        compiler_params=pltpu.CompilerParams(dimension_semantics=("arbitrary",)),
        name="merge",
    )(h, oa, ob, sa, sb, pa, pb, wo)


def _rot_cols(w):
    k, n = w.shape
    w4 = w.reshape(k, n // QK_ROPE, 2, QK_ROPE // 2)
    return jnp.concatenate([-w4[:, :, 1], w4[:, :, 0]], axis=-1).reshape(k, n)


def _mixer_weights(mix_norm, w_in, conv_w, a_log, dt_bias, q_a_norm, w_q_up, kv_a_norm, w_kv_up):
    d = w_in.shape[0]
    qk_w = GDN_HEADS * GDN_DK
    v_w = GDN_HEADS * GDN_DV
    widths = (qk_w, qk_w, v_w, v_w, GDN_HEADS, GDN_HEADS, Q_LORA, KV_LORA, QK_ROPE, d, d)
    offs = np.concatenate([[0], np.cumsum(widths)])
    seg = [w_in[:, int(offs[i]):int(offs[i + 1])] for i in range(len(widths))]
    wqkv = jnp.concatenate(seg[0:3], axis=1).astype(BF)
    wz = seg[3].astype(BF)
    wbg = jnp.concatenate(seg[4:6], axis=1).astype(BF)
    wqd = seg[6].astype(BF)
    wckv = seg[7].astype(BF)
    wkpe = jnp.concatenate([seg[8], _rot_cols(seg[8])], axis=1).astype(BF)
    wga = seg[9].astype(BF)
    wgb = seg[10].astype(BF)
    zeros = jnp.zeros((GDN_HEADS,), F32)
    gp = jnp.stack([jnp.concatenate([zeros, a_log]), jnp.concatenate([zeros, dt_bias])])
    wq = w_q_up.reshape(Q_LORA, MLA_HEADS, QK_NOPE + QK_ROPE)
    wqn = wq[:, :, :QK_NOPE].reshape(Q_LORA, MLA_HEADS * QK_NOPE).astype(BF)
    wqp_f = wq[:, :, QK_NOPE:].reshape(Q_LORA, MLA_HEADS * QK_ROPE)
    wqp = wqp_f.astype(BF)
    wqpr = _rot_cols(wqp_f).astype(BF)
    wkv = w_kv_up.reshape(KV_LORA, MLA_HEADS, QK_NOPE + V_HEAD)
    wkn = wkv[:, :, :QK_NOPE].reshape(KV_LORA, MLA_HEADS * QK_NOPE).astype(BF)
    wv = wkv[:, :, QK_NOPE:].reshape(KV_LORA, MLA_HEADS * V_HEAD).astype(BF)
    inv_freq = ROPE_THETA ** (-jnp.arange(0, QK_ROPE, 2, dtype=F32) / QK_ROPE)
    invf = jnp.tile(inv_freq, LANES // (QK_ROPE // 2))[None, :]
    return (mix_norm[None, :], wqkv, conv_w, wz, wbg, gp, wqd, wckv, wkpe, wga, wgb,
            q_a_norm[None, :], wqn, wqp, wqpr, kv_a_norm[None, :], wkn, wv, invf)


def kernel(x, positions, ffn1_norm, ffn1_w_gate, ffn1_w_up, ffn1_w_down, mix_norm, w_in, conv_w, a_log, dt_bias, gdn_norm, proj_a, q_a_norm, w_q_up, kv_a_norm, w_kv_up, proj_b, w_o, ffn2_norm, ffn2_w_gate, ffn2_w_up, ffn2_w_down, final_norm):
    b, t, d = x.shape
    n = b * t
    depth = ffn1_norm.shape[0]
    tm_ffn = min(512, n)
    tm_in = min(256, t)
    tc = min(256, t)
    tq = min(512, t)
    ones = jnp.ones((1, d), F32)
    pos = positions.reshape(b, t, 1)
    h = x.reshape(n, d)
    for l in range(depth):
        last = l == depth - 1
        h = _ffn(h, ffn1_norm[l][None, :], ffn1_w_gate[l].astype(BF), ffn1_w_up[l].astype(BF),
                 ffn1_w_down[l].astype(BF), ones, final=False, tm=tm_ffn)
        mw = _mixer_weights(mix_norm[l], w_in[l], conv_w[l], a_log[l], dt_bias[l], q_a_norm[l],
                            w_q_up[l], kv_a_norm[l], w_kv_up[l])
        qg, kg, vg, z, bg, qm, km, vm, sa, sb = _inproj(pos, h.reshape(b, t, d), mw, tm=tm_in)
        oa = _gdn(qg, kg, vg, z, bg, gdn_norm[l][None, :], tc=tc)
        ob = _attn(qm, km, vm, tq=tq)
        h = _merge(h, oa.reshape(n, d), ob.reshape(n, d), sa.reshape(n, d), sb.reshape(n, d),
                   proj_a[l].astype(BF), proj_b[l].astype(BF), w_o[l].astype(BF), tm=tm_ffn)
        h = _ffn(h, ffn2_norm[l][None, :], ffn2_w_gate[l].astype(BF), ffn2_w_up[l].astype(BF),
                 ffn2_w_down[l].astype(BF), final_norm[None, :] if last else ones,
                 final=last, tm=tm_ffn)
    return h.reshape(b, t, d)
```

```python
import functools

import numpy as np
import jax
import jax.numpy as jnp
from jax import lax
from jax.experimental import pallas as pl
from jax.experimental.pallas import tpu as pltpu

EPS = 1e-6
CONV_W = 4
GDN_HEADS = 8
GDN_DK = 128
GDN_DV = 128
GDN_CHUNK = 64
MLA_HEADS = 8
Q_LORA = 384
KV_LORA = 256
QK_NOPE = 128
QK_ROPE = 64
V_HEAD = 128
ROPE_THETA = 10000.0
LANES = 128
CONV_HALO = 8

BF = jnp.bfloat16
F32 = jnp.float32
NEG_BIG = -1e30


def _dot(a, b):
    return jnp.dot(a, b, preferred_element_type=F32)


def _dot_nt(a, b):
    return lax.dot_general(a, b, (((1,), (1,)), ((), ())), preferred_element_type=F32)


def _dot_tn(a, b):
    return lax.dot_general(a, b, (((0,), (0,)), ((), ())), preferred_element_type=F32)


def _rms(x, w):
    return x * lax.rsqrt(jnp.mean(x * x, axis=-1, keepdims=True) + EPS) * w


def _sigmoid(x):
    return 1.0 / (1.0 + jnp.exp(-x))


def _resident(shape):
    nd = len(shape)
    return pl.BlockSpec(shape, lambda *_: (0,) * nd, pipeline_mode=pl.Buffered(1))


def _ffn_kernel(x_ref, nw_ref, wg_ref, wu_ref, wd_ref, fw_ref, o_ref, *, final):
    x = x_ref[...]
    xn = _rms(x, nw_ref[...]).astype(BF)
    g = _dot(xn, wg_ref[...])
    u = _dot(xn, wu_ref[...])
    a = (g * _sigmoid(g) * u).astype(BF)
    h = x + 0.5 * _dot(a, wd_ref[...])
    if final:
        h = _rms(h, fw_ref[...])
    o_ref[...] = h


def _ffn(x, nw, wg, wu, wd, fw, *, final, tm):
    n, d = x.shape
    ff = wg.shape[1]
    row = pl.BlockSpec((tm, d), lambda i: (i, 0))
    return pl.pallas_call(
        functools.partial(_ffn_kernel, final=final),
        grid=(n // tm,),
        in_specs=[row, _resident((1, d)), _resident((d, ff)), _resident((d, ff)),
                  _resident((ff, d)), _resident((1, d))],
        out_specs=row,
        out_shape=jax.ShapeDtypeStruct((n, d), F32),
        compiler_params=pltpu.CompilerParams(
            dimension_semantics=("arbitrary",), vmem_limit_bytes=56 * 1024 * 1024),
        name="ffn_final" if final else "ffn",
    )(x, nw, wg, wu, wd, fw)


def _inproj_kernel(pos_ref, h_ref, nw_ref, wqkv_ref, convw_ref, wz_ref, wbg_ref, gp_ref,
                   wqd_ref, wckv_ref, wkpe_ref, wga_ref, wgb_ref, qan_ref, wqn_ref, wqp_ref,
                   wqpr_ref, kvan_ref, wkn_ref, wv_ref, invf_ref,
                   qg_ref, kg_ref, vg_ref, z_ref, bg_ref, qm_ref, km_ref, vm_ref, sa_ref, sb_ref,
                   xbuf_ref, *, tm):
    u = _rms(h_ref[0], nw_ref[...]).astype(BF)

    @pl.when(pl.program_id(1) == 0)
    def _():
        xbuf_ref[0:CONV_HALO, :] = jnp.zeros((CONV_HALO, xbuf_ref.shape[1]), F32)

    xbuf_ref[CONV_HALO:CONV_HALO + tm, :] = _dot(u, wqkv_ref[...])
    n_slab = xbuf_ref.shape[1] // LANES
    per = n_slab // 3
    for s in range(n_slab):
        cols = slice(LANES * s, LANES * (s + 1))
        base = CONV_HALO - (CONV_W - 1)
        acc = convw_ref[0:1, cols] * xbuf_ref[base:base + tm, cols]
        for j in range(1, CONV_W):
            acc = acc + convw_ref[j:j + 1, cols] * xbuf_ref[base + j:base + j + tm, cols]
        y = acc * _sigmoid(acc)
        if s < 2 * per:
            y = y * lax.rsqrt(jnp.sum(y * y, axis=-1, keepdims=True) + EPS)
        if s < per:
            y = y * (GDN_DK ** -0.5)
        dst = (qg_ref, kg_ref, vg_ref)[s // per]
        oc = slice(LANES * (s % per), LANES * (s % per + 1))
        dst[0, :, oc] = y.astype(BF)
    xbuf_ref[0:CONV_HALO, :] = xbuf_ref[tm:tm + CONV_HALO, :]

    zz = _dot(u, wz_ref[...])
    z_ref[0] = (zz * _sigmoid(zz)).astype(BF)

    ba = _dot(u, wbg_ref[...])
    sp_in = ba + gp_ref[1:2, :]
    softplus = jnp.maximum(sp_in, 0.0) + jnp.log1p(jnp.exp(-jnp.abs(sp_in)))
    lane = lax.broadcasted_iota(jnp.int32, ba.shape, 1)
    bg_ref[0] = jnp.where(lane < GDN_HEADS, _sigmoid(ba), -jnp.exp(gp_ref[0:1, :]) * softplus)

    sa_ref[0] = _sigmoid(_dot(u, wga_ref[...])).astype(BF)
    sb_ref[0] = _sigmoid(_dot(u, wgb_ref[...])).astype(BF)

    qn = _rms(_dot(u, wqd_ref[...]), qan_ref[...]).astype(BF)
    cn = _rms(_dot(u, wckv_ref[...]), kvan_ref[...]).astype(BF)
    ang = pos_ref[0].astype(F32) * invf_ref[...]
    cos = jnp.cos(ang)
    sin = jnp.sin(ang)
    reps = (MLA_HEADS * QK_ROPE) // LANES
    cos_h = jnp.concatenate([cos] * reps, axis=-1)
    sin_h = jnp.concatenate([sin] * reps, axis=-1)
    scale = (QK_NOPE + QK_ROPE) ** -0.5 * float(np.log2(np.e))
    q_nope = _dot(qn, wqn_ref[...]) * scale
    q_pe = (_dot(qn, wqp_ref[...]) * cos_h + _dot(qn, wqpr_ref[...]) * sin_h) * scale
    kp = _dot(u, wkpe_ref[...])
    k_pe = (kp[:, :QK_ROPE] * cos[:, :QK_ROPE] + kp[:, QK_ROPE:] * sin[:, :QK_ROPE]).astype(BF)
    k_nope = _dot(cn, wkn_ref[...])
    v = _dot(cn, wv_ref[...])
    for hh in range(MLA_HEADS):
        qm_ref[0, hh, :, 0:QK_NOPE] = q_nope[:, QK_NOPE * hh:QK_NOPE * (hh + 1)].astype(BF)
        qm_ref[0, hh, :, QK_NOPE:] = q_pe[:, QK_ROPE * hh:QK_ROPE * (hh + 1)].astype(BF)
        km_ref[0, hh, :, 0:QK_NOPE] = k_nope[:, QK_NOPE * hh:QK_NOPE * (hh + 1)].astype(BF)
        km_ref[0, hh, :, QK_NOPE:] = k_pe
        vm_ref[0, hh] = v[:, V_HEAD * hh:V_HEAD * (hh + 1)].astype(BF)


def _inproj(pos, h, weights, *, tm):
    b, t, d = h.shape
    nt = t // tm
    conv_ch = weights[1].shape[1]
    qk_w = GDN_HEADS * GDN_DK
    v_w = GDN_HEADS * GDN_DV
    hq = QK_NOPE + QK_ROPE

    def row(width):
        return pl.BlockSpec((1, tm, width), lambda bi, ti: (bi, ti, 0))

    def head(width):
        return pl.BlockSpec((1, MLA_HEADS, tm, width), lambda bi, ti: (bi, 0, ti, 0))

    out_shape = (
        jax.ShapeDtypeStruct((b, t, qk_w), BF), jax.ShapeDtypeStruct((b, t, qk_w), BF),
        jax.ShapeDtypeStruct((b, t, v_w), BF), jax.ShapeDtypeStruct((b, t, v_w), BF),
        jax.ShapeDtypeStruct((b, t, 2 * GDN_HEADS), F32),
        jax.ShapeDtypeStruct((b, MLA_HEADS, t, hq), BF), jax.ShapeDtypeStruct((b, MLA_HEADS, t, hq), BF),
        jax.ShapeDtypeStruct((b, MLA_HEADS, t, V_HEAD), BF),
        jax.ShapeDtypeStruct((b, t, d), BF), jax.ShapeDtypeStruct((b, t, d), BF),
    )
    out_specs = (row(qk_w), row(qk_w), row(v_w), row(v_w), row(2 * GDN_HEADS),
                 head(hq), head(hq), head(V_HEAD), row(d), row(d))
    return pl.pallas_call(
        functools.partial(_inproj_kernel, tm=tm),
        grid=(b, nt),
        in_specs=[row(1), row(d)] + [_resident(w.shape) for w in weights],
        out_specs=out_specs,
        out_shape=out_shape,
        scratch_shapes=[pltpu.VMEM((tm + CONV_HALO, conv_ch), F32)],
        compiler_params=pltpu.CompilerParams(
            dimension_semantics=("arbitrary", "arbitrary"), vmem_limit_bytes=56 * 1024 * 1024),
        name="inproj",
    )(pos, h, *weights)


def _gdn_kernel(q_ref, k_ref, v_ref, z_ref, bg_ref, nw_ref, o_ref, state_ref, *, tc):
    c = GDN_CHUNK

    @pl.when(pl.program_id(1) == 0)
    def _():
        state_ref[...] = jnp.zeros(state_ref.shape, F32)

    ii = lax.broadcasted_iota(jnp.int32, (c, c), 0)
    jj = lax.broadcasted_iota(jnp.int32, (c, c), 1)
    eye = (ii == jj).astype(F32)

    def chunk_body(ci, carry):
        r0 = pl.multiple_of(ci * c, c)
        rows = pl.ds(r0, c)
        bg = bg_ref[0, rows, :]
        heads = range(GDN_HEADS)
        cols = [slice(LANES * hh, LANES * (hh + 1)) for hh in heads]
        q = [q_ref[0, rows, cs] for cs in cols]
        k = [k_ref[0, rows, cs] for cs in cols]
        v = [v_ref[0, rows, cs] for cs in cols]
        beta = [bg[:, hh:hh + 1] for hh in heads]
        g = [bg[:, GDN_HEADS + hh:GDN_HEADS + hh + 1] for hh in heads]
        gc_row = [jnp.sum(jnp.where(ii <= jj, x, 0.0), axis=0, keepdims=True) for x in g]
        gc_col = [jnp.sum(eye * x, axis=1, keepdims=True) for x in gc_row]
        g_last = [jnp.sum(x, axis=0, keepdims=True) for x in g]
        decay = [jnp.where(ii >= jj, jnp.exp(jnp.minimum(gc - gr, 0.0)), 0.0)
                 for gc, gr in zip(gc_col, gc_row)]
        e_col = [jnp.exp(x) for x in gc_col]
        kf = [x.astype(F32) for x in k]
        kb = [x * b for x, b in zip(kf, beta)]
        both = [_dot_nt(jnp.concatenate([kb[hh].astype(BF), q[hh]], axis=0), k[hh]) for hh in heads]
        n = [jnp.where(ii > jj, -both[hh][:c] * decay[hh], 0.0) for hh in heads]
        attn = [(both[hh][c:] * decay[hh]).astype(BF) for hh in heads]
        a = [eye + x for x in n]
        nb = [x.astype(BF) for x in n]
        p = [_dot(x, x) for x in nb]
        for _ in range(4):
            pb = [x.astype(BF) for x in p]
            sq = [_dot(jnp.concatenate([pb[hh], a[hh].astype(BF)], axis=0), pb[hh]) for hh in heads]
            p = [x[:c] for x in sq]
            a = [a[hh] + sq[hh][c:] for hh in heads]
        a = [a[hh] + _dot(a[hh].astype(BF), p[hh].astype(BF)) for hh in heads]
        rhs = [jnp.concatenate([v[hh].astype(F32) * beta[hh], kb[hh] * e_col[hh]], axis=1).astype(BF)
               for hh in heads]
        sol = [_dot(a[hh].astype(BF), rhs[hh]) for hh in heads]
        q_dec = [(q[hh].astype(F32) * e_col[hh]).astype(BF) for hh in heads]
        k_dec = [(kf[hh] * jnp.exp(g_last[hh] - gc_col[hh])).astype(BF) for hh in heads]
        s_old = [state_ref[hh] for hh in heads]
        ws = [_dot(jnp.concatenate([sol[hh][:, GDN_DV:].astype(BF), q_dec[hh]], axis=0),
                   s_old[hh].astype(BF)) for hh in heads]
        v_new = [(sol[hh][:, :GDN_DV] - ws[hh][:c]).astype(BF) for hh in heads]
        o = [ws[hh][c:] + _dot(attn[hh], v_new[hh]) for hh in heads]
        for hh in heads:
            state_ref[hh] = s_old[hh] * jnp.exp(g_last[hh]) + _dot_tn(k_dec[hh], v_new[hh])
        for hh in heads:
            on = _rms(o[hh], nw_ref[...])
            o_ref[0, rows, cols[hh]] = (on * z_ref[0, rows, cols[hh]].astype(F32)).astype(BF)
        return carry

    lax.fori_loop(0, tc // c, chunk_body, 0)


def _gdn(qg, kg, vg, z, bg, nw, *, tc):
    b, t, w = qg.shape
    row = pl.BlockSpec((1, tc, w), lambda bi, ti: (bi, ti, 0))
    return pl.pallas_call(
        functools.partial(_gdn_kernel, tc=tc),
        grid=(b, t // tc),
        in_specs=[row, row, row, row,
                  pl.BlockSpec((1, tc, bg.shape[2]), lambda bi, ti: (bi, ti, 0)),
                  _resident(nw.shape)],
        out_specs=row,
        out_shape=jax.ShapeDtypeStruct((b, t, w), BF),
        scratch_shapes=[pltpu.VMEM((GDN_HEADS, GDN_DK, GDN_DV), F32)],
        compiler_params=pltpu.CompilerParams(dimension_semantics=("arbitrary", "arbitrary")),
        name="gdn",
    )(qg, kg, vg, z, bg, nw)


def _attn_kernel(q_ref, k_ref, v_ref, o_ref, sa_ref, sb_ref, m_ref, l_ref, acc_ref, *, tq):
    t = q_ref.shape[2]
    reps = tq // LANES
    ii = lax.broadcasted_iota(jnp.int32, (tq, tq), 0)
    jj = lax.broadcasted_iota(jnp.int32, (tq, tq), 1)

    def scores(q, kb):
        return _dot_nt(q, k_ref[0, 0, pl.ds(pl.multiple_of(kb * tq, tq), tq), :])

    def update(s_ref, kb):
        s = s_ref[...]
        m_prev = m_ref[...]
        m_next = jnp.maximum(m_prev, jnp.max(s, axis=-1, keepdims=True))
        p = jnp.exp2(s - jnp.concatenate([m_next] * reps, axis=-1))
        alpha = jnp.exp2(m_prev - m_next)
        l_ref[...] = alpha * l_ref[...] + jnp.sum(p, axis=-1, keepdims=True)
        v = v_ref[0, 0, pl.ds(pl.multiple_of(kb * tq, tq), tq), :]
        acc_ref[...] = alpha * acc_ref[...] + _dot(p.astype(BF), v)
        m_ref[...] = m_next

    def q_body(qi, carry):
        q0 = pl.multiple_of(qi * tq, tq)
        q = q_ref[0, 0, pl.ds(q0, tq), :]
        m_ref[...] = jnp.full(m_ref.shape, NEG_BIG, F32)
        l_ref[...] = jnp.zeros(l_ref.shape, F32)
        acc_ref[...] = jnp.zeros(acc_ref.shape, F32)
        sa_ref[...] = jnp.where(jj <= ii, scores(q, qi), NEG_BIG)

        def pair_body(pi, pend):
            sb_ref[...] = scores(q, 2 * pi)
            update(sa_ref, pend)
            sa_ref[...] = scores(q, 2 * pi + 1)
            update(sb_ref, 2 * pi)
            return 2 * pi + 1

        pend = lax.fori_loop(0, qi // 2, pair_body, qi)

        @pl.when(qi % 2 == 1)
        def _():
            sb_ref[...] = scores(q, qi - 1)
            update(sa_ref, pend)
            update(sb_ref, qi - 1)

        @pl.when(qi % 2 == 0)
        def _():
            update(sa_ref, pend)

        o_ref[0, pl.ds(q0, tq), :] = (acc_ref[...] / l_ref[...]).astype(BF)
        return carry

    lax.fori_loop(0, t // tq, q_body, 0)


def _attn(qm, km, vm, *, tq):
    b, h, t, hq = qm.shape
    dv = vm.shape[3]
    return pl.pallas_call(
        functools.partial(_attn_kernel, tq=tq),
        grid=(b, h),
        in_specs=[pl.BlockSpec((1, 1, t, hq), lambda bi, hi: (bi, hi, 0, 0)),
                  pl.BlockSpec((1, 1, t, hq), lambda bi, hi: (bi, hi, 0, 0)),
                  pl.BlockSpec((1, 1, t, dv), lambda bi, hi: (bi, hi, 0, 0))],
        out_specs=pl.BlockSpec((1, t, dv), lambda bi, hi: (bi, 0, hi)),
        out_shape=jax.ShapeDtypeStruct((b, t, h * dv), BF),
        scratch_shapes=[pltpu.VMEM((tq, tq), F32), pltpu.VMEM((tq, tq), F32),
                        pltpu.VMEM((tq, LANES), F32), pltpu.VMEM((tq, LANES), F32),
                        pltpu.VMEM((tq, dv), F32)],
        compiler_params=pltpu.CompilerParams(dimension_semantics=("arbitrary", "arbitrary")),
        name="attn",
    )(qm, km, vm)


def _merge_kernel(h_ref, oa_ref, ob_ref, sa_ref, sb_ref, pa_ref, pb_ref, wo_ref, o_ref):
    ya = _dot(oa_ref[...], pa_ref[...])
    yb = _dot(ob_ref[...], pb_ref[...])
    merged = sa_ref[...].astype(F32) * ya + sb_ref[...].astype(F32) * yb
    o_ref[...] = h_ref[...] + _dot(merged.astype(BF), wo_ref[...])


def _merge(h, oa, ob, sa, sb, pa, pb, wo, *, tm):
    n, d = h.shape
    row = pl.BlockSpec((tm, d), lambda i: (i, 0))
    return pl.pallas_call(
        _merge_kernel,
        grid=(n // tm,),
        in_specs=[row, row, row, row, row, _resident(pa.shape), _resident(pb.shape),
                  _resident(wo.shape)],
        out_specs=row,
        out_shape=jax.ShapeDtypeStruct((n, d), F32),
        compiler_params=pltpu.CompilerParams(dimension_semantics=("arbitrary",)),
        name="merge",
    )(h, oa, ob, sa, sb, pa, pb, wo)


def _rot_cols(w):
    k, n = w.shape
    w4 = w.reshape(k, n // QK_ROPE, 2, QK_ROPE // 2)
    return jnp.concatenate([-w4[:, :, 1], w4[:, :, 0]], axis=-1).reshape(k, n)


def _mixer_weights(mix_norm, w_in, conv_w, a_log, dt_bias, q_a_norm, w_q_up, kv_a_norm, w_kv_up):
    d = w_in.shape[0]
    qk_w = GDN_HEADS * GDN_DK
    v_w = GDN_HEADS * GDN_DV
    widths = (qk_w, qk_w, v_w, v_w, GDN_HEADS, GDN_HEADS, Q_LORA, KV_LORA, QK_ROPE, d, d)
    offs = np.concatenate([[0], np.cumsum(widths)])
    seg = [w_in[:, int(offs[i]):int(offs[i + 1])] for i in range(len(widths))]
    wqkv = jnp.concatenate(seg[0:3], axis=1).astype(BF)
    wz = seg[3].astype(BF)
    wbg = jnp.concatenate(seg[4:6], axis=1).astype(BF)
    wqd = seg[6].astype(BF)
    wckv = seg[7].astype(BF)
    wkpe = jnp.concatenate([seg[8], _rot_cols(seg[8])], axis=1).astype(BF)
    wga = seg[9].astype(BF)
    wgb = seg[10].astype(BF)
    zeros = jnp.zeros((GDN_HEADS,), F32)
    gp = jnp.stack([jnp.concatenate([zeros, a_log]), jnp.concatenate([zeros, dt_bias])])
    wq = w_q_up.reshape(Q_LORA, MLA_HEADS, QK_NOPE + QK_ROPE)
    wqn = wq[:, :, :QK_NOPE].reshape(Q_LORA, MLA_HEADS * QK_NOPE).astype(BF)
    wqp_f = wq[:, :, QK_NOPE:].reshape(Q_LORA, MLA_HEADS * QK_ROPE)
    wqp = wqp_f.astype(BF)
    wqpr = _rot_cols(wqp_f).astype(BF)
    wkv = w_kv_up.reshape(KV_LORA, MLA_HEADS, QK_NOPE + V_HEAD)
    wkn = wkv[:, :, :QK_NOPE].reshape(KV_LORA, MLA_HEADS * QK_NOPE).astype(BF)
    wv = wkv[:, :, QK_NOPE:].reshape(KV_LORA, MLA_HEADS * V_HEAD).astype(BF)
    inv_freq = ROPE_THETA ** (-jnp.arange(0, QK_ROPE, 2, dtype=F32) / QK_ROPE)
    invf = jnp.tile(inv_freq, LANES // (QK_ROPE // 2))[None, :]
    return (mix_norm[None, :], wqkv, conv_w, wz, wbg, gp, wqd, wckv, wkpe, wga, wgb,
            q_a_norm[None, :], wqn, wqp, wqpr, kv_a_norm[None, :], wkn, wv, invf)


def kernel(x, positions, ffn1_norm, ffn1_w_gate, ffn1_w_up, ffn1_w_down, mix_norm, w_in, conv_w, a_log, dt_bias, gdn_norm, proj_a, q_a_norm, w_q_up, kv_a_norm, w_kv_up, proj_b, w_o, ffn2_norm, ffn2_w_gate, ffn2_w_up, ffn2_w_down, final_norm):
    b, t, d = x.shape
    n = b * t
    depth = ffn1_norm.shape[0]
    tm_ffn = min(512, n)
    tm_in = min(256, t)
    tc = min(256, t)
    tq = min(512, t)
    ones = jnp.ones((1, d), F32)
    pos = positions.reshape(b, t, 1)
    h = x.reshape(n, d)
    for l in range(depth):
        last = l == depth - 1
        h = _ffn(h, ffn1_norm[l][None, :], ffn1_w_gate[l].astype(BF), ffn1_w_up[l].astype(BF),
                 ffn1_w_down[l].astype(BF), ones, final=False, tm=tm_ffn)
        mw = _mixer_weights(mix_norm[l], w_in[l], conv_w[l], a_log[l], dt_bias[l], q_a_norm[l],
                            w_q_up[l], kv_a_norm[l], w_kv_up[l])
        qg, kg, vg, z, bg, qm, km, vm, sa, sb = _inproj(pos, h.reshape(b, t, d), mw, tm=tm_in)
        oa = _gdn(qg, kg, vg, z, bg, gdn_norm[l][None, :], tc=tc)
        ob = _attn(qm, km, vm, tq=tq)
        h = _merge(h, oa.reshape(n, d), ob.reshape(n, d), sa.reshape(n, d), sb.reshape(n, d),
                   proj_a[l].astype(BF), proj_b[l].astype(BF), w_o[l].astype(BF), tm=tm_ffn)
        h = _ffn(h, ffn2_norm[l][None, :], ffn2_w_gate[l].astype(BF), ffn2_w_up[l].astype(BF),
                 ffn2_w_down[l].astype(BF), final_norm[None, :] if last else ones,
                 final=last, tm=tm_ffn)
    return h.reshape(b, t, d)
```

```python
import functools

import numpy as np
import jax
import jax.numpy as jnp
from jax import lax
from jax.experimental import pallas as pl
from jax.experimental.pallas import tpu as pltpu

EPS = 1e-6
CONV_W = 4
GDN_HEADS = 8
GDN_DK = 128
GDN_DV = 128
GDN_CHUNK = 64
MLA_HEADS = 8
Q_LORA = 384
KV_LORA = 256
QK_NOPE = 128
QK_ROPE = 64
V_HEAD = 128
ROPE_THETA = 10000.0
LANES = 128
CONV_HALO = 8
assert CONV_HALO >= CONV_W - 1

BF = jnp.bfloat16
F32 = jnp.float32
NEG_BIG = -1e30


def _dot(a, b):
    return jnp.dot(a, b, preferred_element_type=F32)


def _dot_nt(a, b):
    return lax.dot_general(a, b, (((1,), (1,)), ((), ())), preferred_element_type=F32)


def _dot_tn(a, b):
    return lax.dot_general(a, b, (((0,), (0,)), ((), ())), preferred_element_type=F32)


def _rms(x, w):
    return x * lax.rsqrt(jnp.mean(x * x, axis=-1, keepdims=True) + EPS) * w


def _sigmoid(x):
    return 0.5 * jnp.tanh(0.5 * x) + 0.5


def _silu(x):
    h = 0.5 * x
    return h * jnp.tanh(h) + h


def _resident(shape):
    nd = len(shape)
    return pl.BlockSpec(shape, lambda *_: (0,) * nd, pipeline_mode=pl.Buffered(1))


def _ffn_kernel(x_ref, nw_ref, wg_ref, wu_ref, wd_ref, fw_ref, o_ref, *, final):
    x = x_ref[...]
    xn = _rms(x, nw_ref[...]).astype(BF)
    g = _dot(xn, wg_ref[...])
    u = _dot(xn, wu_ref[...])
    a = (_silu(g) * u).astype(BF)
    h = x + 0.5 * _dot(a, wd_ref[...])
    if final:
        h = _rms(h, fw_ref[...])
    o_ref[...] = h


def _ffn(x, nw, wg, wu, wd, fw, *, final, tm):
    n, d = x.shape
    ff = wg.shape[1]
    row = pl.BlockSpec((tm, d), lambda i: (i, 0))
    return pl.pallas_call(
        functools.partial(_ffn_kernel, final=final),
        grid=(n // tm,),
        in_specs=[row, _resident((1, d)), _resident((d, ff)), _resident((d, ff)),
                  _resident((ff, d)), _resident((1, d))],
        out_specs=row,
        out_shape=jax.ShapeDtypeStruct((n, d), F32),
        compiler_params=pltpu.CompilerParams(
            dimension_semantics=("arbitrary",), vmem_limit_bytes=56 * 1024 * 1024),
        name="ffn_final" if final else "ffn",
    )(x, nw, wg, wu, wd, fw)


def _inproj_kernel(pos_ref, h_ref, nw_ref, wqkv_ref, convw_ref, wz_ref, wbg_ref, gp_ref,
                   wqd_ref, wckv_ref, wkpe_ref, wga_ref, wgb_ref, qan_ref, wqn_ref, wqp_ref,
                   wqpr_ref, kvan_ref, wkn_ref, wv_ref, invf_ref,
                   qg_ref, kg_ref, vg_ref, z_ref, bg_ref, qm_ref, km_ref, vm_ref, sa_ref, sb_ref,
                   xbuf_ref, *, tm):
    u = _rms(h_ref[0], nw_ref[...]).astype(BF)

    @pl.when(pl.program_id(1) == 0)
    def _():
        xbuf_ref[0:CONV_HALO, :] = jnp.zeros((CONV_HALO, xbuf_ref.shape[1]), F32)

    xbuf_ref[CONV_HALO:CONV_HALO + tm, :] = _dot(u, wqkv_ref[...])
    n_slab = xbuf_ref.shape[1] // LANES
    per = n_slab // 3
    for s in range(n_slab):
        cols = slice(LANES * s, LANES * (s + 1))
        xs = xbuf_ref[:, cols]
        acc = convw_ref[CONV_W - 1:CONV_W, cols] * xs
        for j in range(CONV_W - 1):
            acc = acc + pltpu.roll(convw_ref[j:j + 1, cols] * xs, CONV_W - 1 - j, 0)
        y = _silu(acc[CONV_HALO:, :])
        if s < 2 * per:
            y = y * lax.rsqrt(jnp.sum(y * y, axis=-1, keepdims=True) + EPS)
        if s < per:
            y = y * (GDN_DK ** -0.5)
        dst = (qg_ref, kg_ref, vg_ref)[s // per]
        oc = slice(LANES * (s % per), LANES * (s % per + 1))
        dst[0, :, oc] = y.astype(BF)
    xbuf_ref[0:CONV_HALO, :] = xbuf_ref[tm:tm + CONV_HALO, :]

    zz = _dot(u, wz_ref[...])
    z_ref[0] = _silu(zz).astype(BF)

    ba = _dot(u, wbg_ref[...])
    sp_in = ba + gp_ref[1:2, :]
    softplus = jnp.maximum(sp_in, 0.0) + jnp.log1p(jnp.exp(-jnp.abs(sp_in)))
    lane = lax.broadcasted_iota(jnp.int32, ba.shape, 1)
    bg_ref[0] = jnp.where(lane < GDN_HEADS, _sigmoid(ba), -jnp.exp(gp_ref[0:1, :]) * softplus)

    sa_ref[0] = _sigmoid(_dot(u, wga_ref[...])).astype(BF)
    sb_ref[0] = _sigmoid(_dot(u, wgb_ref[...])).astype(BF)

    qn = _rms(_dot(u, wqd_ref[...]), qan_ref[...]).astype(BF)
    cn = _rms(_dot(u, wckv_ref[...]), kvan_ref[...]).astype(BF)
    ang = pos_ref[0].astype(F32) * invf_ref[...]
    cos = jnp.cos(ang)
    sin = jnp.sin(ang)
    reps = (MLA_HEADS * QK_ROPE) // LANES
    cos_h = jnp.concatenate([cos] * reps, axis=-1)
    sin_h = jnp.concatenate([sin] * reps, axis=-1)
    scale = (QK_NOPE + QK_ROPE) ** -0.5 * float(np.log2(np.e))
    q_nope = _dot(qn, wqn_ref[...]) * scale
    q_pe = (_dot(qn, wqp_ref[...]) * cos_h + _dot(qn, wqpr_ref[...]) * sin_h) * scale
    kp = _dot(u, wkpe_ref[...])
    k_pe = (kp[:, :QK_ROPE] * cos[:, :QK_ROPE] + kp[:, QK_ROPE:] * sin[:, :QK_ROPE]).astype(BF)
    k_nope = _dot(cn, wkn_ref[...])
    v = _dot(cn, wv_ref[...])
    for hh in range(MLA_HEADS):
        qm_ref[0, hh, :, 0:QK_NOPE] = q_nope[:, QK_NOPE * hh:QK_NOPE * (hh + 1)].astype(BF)
        qm_ref[0, hh, :, QK_NOPE:] = q_pe[:, QK_ROPE * hh:QK_ROPE * (hh + 1)].astype(BF)
        km_ref[0, hh, :, 0:QK_NOPE] = k_nope[:, QK_NOPE * hh:QK_NOPE * (hh + 1)].astype(BF)
        km_ref[0, hh, :, QK_NOPE:] = k_pe
        vm_ref[0, hh] = v[:, V_HEAD * hh:V_HEAD * (hh + 1)].astype(BF)


def _inproj(pos, h, weights, *, tm):
    b, t, d = h.shape
    nt = t // tm
    conv_ch = weights[1].shape[1]
    qk_w = GDN_HEADS * GDN_DK
    v_w = GDN_HEADS * GDN_DV
    hq = QK_NOPE + QK_ROPE

    def row(width):
        return pl.BlockSpec((1, tm, width), lambda bi, ti: (bi, ti, 0))

    def head(width):
        return pl.BlockSpec((1, MLA_HEADS, tm, width), lambda bi, ti: (bi, 0, ti, 0))

    out_shape = (
        jax.ShapeDtypeStruct((b, t, qk_w), BF), jax.ShapeDtypeStruct((b, t, qk_w), BF),
        jax.ShapeDtypeStruct((b, t, v_w), BF), jax.ShapeDtypeStruct((b, t, v_w), BF),
        jax.ShapeDtypeStruct((b, t, 2 * GDN_HEADS), F32),
        jax.ShapeDtypeStruct((b, MLA_HEADS, t, hq), BF), jax.ShapeDtypeStruct((b, MLA_HEADS, t, hq), BF),
        jax.ShapeDtypeStruct((b, MLA_HEADS, t, V_HEAD), BF),
        jax.ShapeDtypeStruct((b, t, d), BF), jax.ShapeDtypeStruct((b, t, d), BF),
    )
    out_specs = (row(qk_w), row(qk_w), row(v_w), row(v_w), row(2 * GDN_HEADS),
                 head(hq), head(hq), head(V_HEAD), row(d), row(d))
    return pl.pallas_call(
        functools.partial(_inproj_kernel, tm=tm),
        grid=(b, nt),
        in_specs=[row(1), row(d)] + [_resident(w.shape) for w in weights],
        out_specs=out_specs,
        out_shape=out_shape,
        scratch_shapes=[pltpu.VMEM((tm + CONV_HALO, conv_ch), F32)],
        compiler_params=pltpu.CompilerParams(
            dimension_semantics=("arbitrary", "arbitrary"), vmem_limit_bytes=56 * 1024 * 1024),
        name="inproj",
    )(pos, h, *weights)


def _gdn_kernel(q_ref, k_ref, v_ref, z_ref, bg_ref, nw_ref, o_ref, state_ref, *, tc):
    c = GDN_CHUNK

    @pl.when(pl.program_id(1) == 0)
    def _():
        state_ref[...] = jnp.zeros(state_ref.shape, F32)

    ii = lax.broadcasted_iota(jnp.int32, (c, c), 0)
    jj = lax.broadcasted_iota(jnp.int32, (c, c), 1)
    eye = (ii == jj).astype(F32)
    heads = range(GDN_HEADS)
    nchunk = tc // c
    probs = [(ci, hh) for ci in range(nchunk) for hh in heads]
    rows = [slice(ci * c, (ci + 1) * c) for ci, _ in probs]
    cols = [slice(LANES * hh, LANES * (hh + 1)) for _, hh in probs]
    ids = range(len(probs))
    bgs = [bg_ref[0, ci * c:(ci + 1) * c, :] for ci in range(nchunk)]
    q = [q_ref[0, rows[i], cols[i]] for i in ids]
    k = [k_ref[0, rows[i], cols[i]] for i in ids]
    v = [v_ref[0, rows[i], cols[i]] for i in ids]
    beta = [bgs[ci][:, hh:hh + 1] for ci, hh in probs]
    g = [bgs[ci][:, GDN_HEADS + hh:GDN_HEADS + hh + 1] for ci, hh in probs]
    gc_row = [jnp.sum(jnp.where(ii <= jj, x, 0.0), axis=0, keepdims=True) for x in g]
    gc_col = [jnp.sum(eye * x, axis=1, keepdims=True) for x in gc_row]
    g_last = [jnp.sum(x, axis=0, keepdims=True) for x in g]
    decay = [jnp.where(ii >= jj, jnp.exp(jnp.minimum(gc - gr, 0.0)), 0.0)
             for gc, gr in zip(gc_col, gc_row)]
    e_col = [jnp.exp(x) for x in gc_col]
    kf = [x.astype(F32) for x in k]
    kb = [x * b for x, b in zip(kf, beta)]
    both = [_dot_nt(jnp.concatenate([kb[i].astype(BF), q[i]], axis=0), k[i]) for i in ids]
    n = [jnp.where(ii > jj, -both[i][:c] * decay[i], 0.0) for i in ids]
    attn = [(both[i][c:] * decay[i]).astype(BF) for i in ids]
    a = [eye + x for x in n]
    nb = [x.astype(BF) for x in n]
    p = [_dot(x, x) for x in nb]
    for _ in range(4):
        pb = [x.astype(BF) for x in p]
        sq = [_dot(jnp.concatenate([pb[i], a[i].astype(BF)], axis=0), pb[i]) for i in ids]
        p = [x[:c] for x in sq]
        a = [a[i] + sq[i][c:] for i in ids]
    a = [a[i] + _dot(a[i].astype(BF), p[i].astype(BF)) for i in ids]
    rhs = [jnp.concatenate([v[i].astype(F32) * beta[i], kb[i] * e_col[i]], axis=1).astype(BF)
           for i in ids]
    sol = [_dot(a[i].astype(BF), rhs[i]) for i in ids]
    u = [x[:, :GDN_DV] for x in sol]
    wq = [jnp.concatenate([sol[i][:, GDN_DV:].astype(BF), (q[i].astype(F32) * e_col[i]).astype(BF)],
                          axis=0) for i in ids]
    k_dec = [(kf[i] * jnp.exp(g_last[i] - gc_col[i])).astype(BF) for i in ids]
    e_last = [jnp.exp(x) for x in g_last]
    state = [state_ref[hh] for hh in heads]
    for ci in range(nchunk):
        idx = [ci * GDN_HEADS + hh for hh in heads]
        ws = [_dot(wq[i], state[hh].astype(BF)) for hh, i in zip(heads, idx)]
        v_new = [(u[i] - ws[hh][:c]).astype(BF) for hh, i in zip(heads, idx)]
        o = [ws[hh][c:] + _dot(attn[i], v_new[hh]) for hh, i in zip(heads, idx)]
        state = [state[hh] * e_last[i] + _dot_tn(k_dec[i], v_new[hh]) for hh, i in zip(heads, idx)]
        for hh, i in zip(heads, idx):
            on = _rms(o[hh], nw_ref[...])
            o_ref[0, rows[i], cols[i]] = (on * z_ref[0, rows[i], cols[i]].astype(F32)).astype(BF)
    for hh in heads:
        state_ref[hh] = state[hh]


def _gdn(qg, kg, vg, z, bg, nw, *, tc):
    b, t, w = qg.shape
    row = pl.BlockSpec((1, tc, w), lambda bi, ti: (bi, ti, 0))
    return pl.pallas_call(
        functools.partial(_gdn_kernel, tc=tc),
        grid=(b, t // tc),
        in_specs=[row, row, row, row,
                  pl.BlockSpec((1, tc, bg.shape[2]), lambda bi, ti: (bi, ti, 0)),
                  _resident(nw.shape)],
        out_specs=row,
        out_shape=jax.ShapeDtypeStruct((b, t, w), BF),
        scratch_shapes=[pltpu.VMEM((GDN_HEADS, GDN_DK, GDN_DV), F32)],
        compiler_params=pltpu.CompilerParams(dimension_semantics=("arbitrary", "arbitrary")),
        name="gdn",
    )(qg, kg, vg, z, bg, nw)


def _attn_kernel(q_ref, k_ref, v_ref, o_ref, sa_ref, sb_ref, m_ref, l_ref, acc_ref, *, tq):
    t = q_ref.shape[2]
    reps = tq // LANES
    ii = lax.broadcasted_iota(jnp.int32, (tq, tq), 0)
    jj = lax.broadcasted_iota(jnp.int32, (tq, tq), 1)

    def scores(q, kb):
        return _dot_nt(q, k_ref[0, 0, pl.ds(pl.multiple_of(kb * tq, tq), tq), :])

    def update(s_ref, kb):
        s = s_ref[...]
        m_prev = m_ref[...]
        m_next = jnp.maximum(m_prev, jnp.max(s, axis=-1, keepdims=True))
        p = jnp.exp2(s - jnp.concatenate([m_next] * reps, axis=-1))
        alpha = jnp.exp2(m_prev - m_next)
        l_ref[...] = alpha * l_ref[...] + jnp.sum(p, axis=-1, keepdims=True)
        v = v_ref[0, 0, pl.ds(pl.multiple_of(kb * tq, tq), tq), :]
        acc_ref[...] = alpha * acc_ref[...] + _dot(p.astype(BF), v)
        m_ref[...] = m_next

    def q_body(qi, carry):
        q0 = pl.multiple_of(qi * tq, tq)
        q = q_ref[0, 0, pl.ds(q0, tq), :]
        m_ref[...] = jnp.full(m_ref.shape, NEG_BIG, F32)
        l_ref[...] = jnp.zeros(l_ref.shape, F32)
        acc_ref[...] = jnp.zeros(acc_ref.shape, F32)
        sa_ref[...] = jnp.where(jj <= ii, scores(q, qi), NEG_BIG)

        def pair_body(pi, pend):
            sb_ref[...] = scores(q, 2 * pi)
            update(sa_ref, pend)
            sa_ref[...] = scores(q, 2 * pi + 1)
            update(sb_ref, 2 * pi)
            return 2 * pi + 1

        pend = lax.fori_loop(0, qi // 2, pair_body, qi)

        @pl.when(qi % 2 == 1)
        def _():
            sb_ref[...] = scores(q, qi - 1)
            update(sa_ref, pend)
            update(sb_ref, qi - 1)

        @pl.when(qi % 2 == 0)
        def _():
            update(sa_ref, pend)

        o_ref[0, pl.ds(q0, tq), :] = (acc_ref[...] / l_ref[...]).astype(BF)
        return carry

    lax.fori_loop(0, t // tq, q_body, 0)


def _attn(qm, km, vm, *, tq):
    b, h, t, hq = qm.shape
    dv = vm.shape[3]
    return pl.pallas_call(
        functools.partial(_attn_kernel, tq=tq),
        grid=(b, h),
        in_specs=[pl.BlockSpec((1, 1, t, hq), lambda bi, hi: (bi, hi, 0, 0)),
                  pl.BlockSpec((1, 1, t, hq), lambda bi, hi: (bi, hi, 0, 0)),
                  pl.BlockSpec((1, 1, t, dv), lambda bi, hi: (bi, hi, 0, 0))],
        out_specs=pl.BlockSpec((1, t, dv), lambda bi, hi: (bi, 0, hi)),
        out_shape=jax.ShapeDtypeStruct((b, t, h * dv), BF),
        scratch_shapes=[pltpu.VMEM((tq, tq), F32), pltpu.VMEM((tq, tq), F32),
                        pltpu.VMEM((tq, LANES), F32), pltpu.VMEM((tq, LANES), F32),
                        pltpu.VMEM((tq, dv), F32)],
        compiler_params=pltpu.CompilerParams(dimension_semantics=("arbitrary", "arbitrary")),
        name="attn",
    )(qm, km, vm)


def _merge_kernel(h_ref, oa_ref, ob_ref, sa_ref, sb_ref, pa_ref, pb_ref, wo_ref, o_ref):
    ya = _dot(oa_ref[...], pa_ref[...])
    yb = _dot(ob_ref[...], pb_ref[...])
    merged = sa_ref[...].astype(F32) * ya + sb_ref[...].astype(F32) * yb
    o_ref[...] = h_ref[...] + _dot(merged.astype(BF), wo_ref[...])


def _merge(h, oa, ob, sa, sb, pa, pb, wo, *, tm):
    n, d = h.shape
    row = pl.BlockSpec((tm, d), lambda i: (i, 0))
    return pl.pallas_call(
        _merge_kernel,
        grid=(n // tm,),
        in_specs=[row, row, row, row, row, _resident(pa.shape), _resident(pb.shape),
                  _resident(wo.shape)],
        out_specs=row,
        out_shape=jax.ShapeDtypeStruct((n, d), F32),
        compiler_params=pltpu.CompilerParams(dimension_semantics=("arbitrary",)),
        name="merge",
    )(h, oa, ob, sa, sb, pa, pb, wo)


def _rot_cols(w):
    k, n = w.shape
    w4 = w.reshape(k, n // QK_ROPE, 2, QK_ROPE // 2)
    return jnp.concatenate([-w4[:, :, 1], w4[:, :, 0]], axis=-1).reshape(k, n)


def _mixer_weights(mix_norm, w_in, conv_w, a_log, dt_bias, q_a_norm, w_q_up, kv_a_norm, w_kv_up):
    d = w_in.shape[0]
    qk_w = GDN_HEADS * GDN_DK
    v_w = GDN_HEADS * GDN_DV
    widths = (qk_w, qk_w, v_w, v_w, GDN_HEADS, GDN_HEADS, Q_LORA, KV_LORA, QK_ROPE, d, d)
    offs = np.concatenate([[0], np.cumsum(widths)])
    seg = [w_in[:, int(offs[i]):int(offs[i + 1])] for i in range(len(widths))]
    wqkv = jnp.concatenate(seg[0:3], axis=1).astype(BF)
    wz = seg[3].astype(BF)
    wbg = jnp.concatenate(seg[4:6], axis=1).astype(BF)
    wqd = seg[6].astype(BF)
    wckv = seg[7].astype(BF)
    wkpe = jnp.concatenate([seg[8], _rot_cols(seg[8])], axis=1).astype(BF)
    wga = seg[9].astype(BF)
    wgb = seg[10].astype(BF)
    zeros = jnp.zeros((GDN_HEADS,), F32)
    gp = jnp.stack([jnp.concatenate([zeros, a_log]), jnp.concatenate([zeros, dt_bias])])
    wq = w_q_up.reshape(Q_LORA, MLA_HEADS, QK_NOPE + QK_ROPE)
    wqn = wq[:, :, :QK_NOPE].reshape(Q_LORA, MLA_HEADS * QK_NOPE).astype(BF)
    wqp_f = wq[:, :, QK_NOPE:].reshape(Q_LORA, MLA_HEADS * QK_ROPE)
    wqp = wqp_f.astype(BF)
    wqpr = _rot_cols(wqp_f).astype(BF)
    wkv = w_kv_up.reshape(KV_LORA, MLA_HEADS, QK_NOPE + V_HEAD)
    wkn = wkv[:, :, :QK_NOPE].reshape(KV_LORA, MLA_HEADS * QK_NOPE).astype(BF)
    wv = wkv[:, :, QK_NOPE:].reshape(KV_LORA, MLA_HEADS * V_HEAD).astype(BF)
    inv_freq = ROPE_THETA ** (-jnp.arange(0, QK_ROPE, 2, dtype=F32) / QK_ROPE)
    invf = jnp.tile(inv_freq, LANES // (QK_ROPE // 2))[None, :]
    return (mix_norm[None, :], wqkv, conv_w, wz, wbg, gp, wqd, wckv, wkpe, wga, wgb,
            q_a_norm[None, :], wqn, wqp, wqpr, kv_a_norm[None, :], wkn, wv, invf)


def kernel(x, positions, ffn1_norm, ffn1_w_gate, ffn1_w_up, ffn1_w_down, mix_norm, w_in, conv_w, a_log, dt_bias, gdn_norm, proj_a, q_a_norm, w_q_up, kv_a_norm, w_kv_up, proj_b, w_o, ffn2_norm, ffn2_w_gate, ffn2_w_up, ffn2_w_down, final_norm):
    b, t, d = x.shape
    n = b * t
    depth = ffn1_norm.shape[0]
    tm_ffn = min(512, n)
    tm_in = min(256, t)
    tc = min(256, t)
    tq = min(512, t)
    ones = jnp.ones((1, d), F32)
    pos = positions.reshape(b, t, 1)
    h = x.reshape(n, d)
    for l in range(depth):
        last = l == depth - 1
        h = _ffn(h, ffn1_norm[l][None, :], ffn1_w_gate[l].astype(BF), ffn1_w_up[l].astype(BF),
                 ffn1_w_down[l].astype(BF), ones, final=False, tm=tm_ffn)
        mw = _mixer_weights(mix_norm[l], w_in[l], conv_w[l], a_log[l], dt_bias[l], q_a_norm[l],
                            w_q_up[l], kv_a_norm[l], w_kv_up[l])
        qg, kg, vg, z, bg, qm, km, vm, sa, sb = _inproj(pos, h.reshape(b, t, d), mw, tm=tm_in)
        oa = _gdn(qg, kg, vg, z, bg, gdn_norm[l][None, :], tc=tc)
        ob = _attn(qm, km, vm, tq=tq)
        h = _merge(h, oa.reshape(n, d), ob.reshape(n, d), sa.reshape(n, d), sb.reshape(n, d),
                   proj_a[l].astype(BF), proj_b[l].astype(BF), w_o[l].astype(BF), tm=tm_ffn)
        h = _ffn(h, ffn2_norm[l][None, :], ffn2_w_gate[l].astype(BF), ffn2_w_up[l].astype(BF),
                 ffn2_w_down[l].astype(BF), final_norm[None, :] if last else ones,
                 final=last, tm=tm_ffn)
    return h.reshape(b, t, d)
```

```python
import functools

import numpy as np
import jax
import jax.numpy as jnp
from jax import lax
from jax.experimental import pallas as pl
from jax.experimental.pallas import tpu as pltpu

EPS = 1e-6
CONV_W = 4
GDN_HEADS = 8
GDN_DK = 128
GDN_DV = 128
GDN_CHUNK = 64
MLA_HEADS = 8
Q_LORA = 384
KV_LORA = 256
QK_NOPE = 128
QK_ROPE = 64
V_HEAD = 128
ROPE_THETA = 10000.0
LANES = 128
CONV_HALO = 8
assert CONV_HALO >= CONV_W - 1

BF = jnp.bfloat16
F32 = jnp.float32
NEG_BIG = -1e30


def _dot(a, b):
    return jnp.dot(a, b, preferred_element_type=F32)


def _dot_nt(a, b):
    return lax.dot_general(a, b, (((1,), (1,)), ((), ())), preferred_element_type=F32)


def _dot_tn(a, b):
    return lax.dot_general(a, b, (((0,), (0,)), ((), ())), preferred_element_type=F32)


def _rms(x, w):
    return x * lax.rsqrt(jnp.mean(x * x, axis=-1, keepdims=True) + EPS) * w


def _sigmoid(x):
    return 0.5 * jnp.tanh(0.5 * x) + 0.5


def _silu(x):
    h = 0.5 * x
    return h * jnp.tanh(h) + h


def _resident(shape):
    nd = len(shape)
    return pl.BlockSpec(shape, lambda *_: (0,) * nd, pipeline_mode=pl.Buffered(1))


def _ffn_kernel(x_ref, nw_ref, wg_ref, wu_ref, wd_ref, fw_ref, o_ref, *, final):
    x = x_ref[...]
    xn = _rms(x, nw_ref[...]).astype(BF)
    g = _dot(xn, wg_ref[...])
    u = _dot(xn, wu_ref[...])
    a = (_silu(g) * u).astype(BF)
    h = x + 0.5 * _dot(a, wd_ref[...])
    if final:
        h = _rms(h, fw_ref[...])
    o_ref[...] = h


def _ffn(x, nw, wg, wu, wd, fw, *, final, tm):
    n, d = x.shape
    ff = wg.shape[1]
    row = pl.BlockSpec((tm, d), lambda i: (i, 0))
    return pl.pallas_call(
        functools.partial(_ffn_kernel, final=final),
        grid=(n // tm,),
        in_specs=[row, _resident((1, d)), _resident((d, ff)), _resident((d, ff)),
                  _resident((ff, d)), _resident((1, d))],
        out_specs=row,
        out_shape=jax.ShapeDtypeStruct((n, d), F32),
        compiler_params=pltpu.CompilerParams(
            dimension_semantics=("arbitrary",), vmem_limit_bytes=56 * 1024 * 1024),
        name="ffn_final" if final else "ffn",
    )(x, nw, wg, wu, wd, fw)


def _inproj_kernel(pos_ref, h_ref, nw_ref, wqkv_ref, convw_ref, wz_ref, wbg_ref, gp_ref,
                   wqd_ref, wckv_ref, wkpe_ref, wga_ref, wgb_ref, qan_ref, wqn_ref, wqp_ref,
                   wqpr_ref, kvan_ref, wkn_ref, wv_ref, invf_ref,
                   qg_ref, kg_ref, vg_ref, z_ref, bg_ref, qm_ref, km_ref, vm_ref, sa_ref, sb_ref,
                   xbuf_ref, *, tm):
    u = _rms(h_ref[0], nw_ref[...]).astype(BF)

    @pl.when(pl.program_id(1) == 0)
    def _():
        xbuf_ref[0:CONV_HALO, :] = jnp.zeros((CONV_HALO, xbuf_ref.shape[1]), F32)

    xbuf_ref[CONV_HALO:CONV_HALO + tm, :] = _dot(u, wqkv_ref[...])
    n_slab = xbuf_ref.shape[1] // LANES
    per = n_slab // 3
    for s in range(n_slab):
        cols = slice(LANES * s, LANES * (s + 1))
        xs = xbuf_ref[:, cols]
        acc = convw_ref[CONV_W - 1:CONV_W, cols] * xs
        for j in range(CONV_W - 1):
            acc = acc + pltpu.roll(convw_ref[j:j + 1, cols] * xs, CONV_W - 1 - j, 0)
        y = _silu(acc[CONV_HALO:, :])
        if s < 2 * per:
            y = y * lax.rsqrt(jnp.sum(y * y, axis=-1, keepdims=True) + EPS)
        if s < per:
            y = y * (GDN_DK ** -0.5)
        dst = (qg_ref, kg_ref, vg_ref)[s // per]
        oc = slice(LANES * (s % per), LANES * (s % per + 1))
        dst[0, :, oc] = y.astype(BF)
    xbuf_ref[0:CONV_HALO, :] = xbuf_ref[tm:tm + CONV_HALO, :]

    zz = _dot(u, wz_ref[...])
    z_ref[0] = _silu(zz).astype(BF)

    ba = _dot(u, wbg_ref[...])
    sp_in = ba + gp_ref[1:2, :]
    softplus = jnp.maximum(sp_in, 0.0) + jnp.log1p(jnp.exp(-jnp.abs(sp_in)))
    lane = lax.broadcasted_iota(jnp.int32, ba.shape, 1)
    bg_ref[0] = jnp.where(lane < GDN_HEADS, _sigmoid(ba), -jnp.exp(gp_ref[0:1, :]) * softplus)

    sa_ref[0] = _sigmoid(_dot(u, wga_ref[...])).astype(BF)
    sb_ref[0] = _sigmoid(_dot(u, wgb_ref[...])).astype(BF)

    qn = _rms(_dot(u, wqd_ref[...]), qan_ref[...]).astype(BF)
    cn = _rms(_dot(u, wckv_ref[...]), kvan_ref[...]).astype(BF)
    pos_rows = jnp.broadcast_to(pos_ref[0].astype(F32), (LANES, tm))
    ang = pos_rows.T * invf_ref[...]
    cos = jnp.cos(ang)
    sin = jnp.sin(ang)
    reps = (MLA_HEADS * QK_ROPE) // LANES
    cos_h = jnp.concatenate([cos] * reps, axis=-1)
    sin_h = jnp.concatenate([sin] * reps, axis=-1)
    scale = (QK_NOPE + QK_ROPE) ** -0.5 * float(np.log2(np.e))
    q_nope = _dot(qn, wqn_ref[...]) * scale
    q_pe = (_dot(qn, wqp_ref[...]) * cos_h + _dot(qn, wqpr_ref[...]) * sin_h) * scale
    kp = _dot(u, wkpe_ref[...])
    k_pe = (kp[:, :QK_ROPE] * cos[:, :QK_ROPE] + kp[:, QK_ROPE:] * sin[:, :QK_ROPE]).astype(BF)
    k_nope = _dot(cn, wkn_ref[...])
    v = _dot(cn, wv_ref[...])
    for hh in range(MLA_HEADS):
        qm_ref[0, hh, :, 0:QK_NOPE] = q_nope[:, QK_NOPE * hh:QK_NOPE * (hh + 1)].astype(BF)
        qm_ref[0, hh, :, QK_NOPE:] = q_pe[:, QK_ROPE * hh:QK_ROPE * (hh + 1)].astype(BF)
        km_ref[0, hh, :, 0:QK_NOPE] = k_nope[:, QK_NOPE * hh:QK_NOPE * (hh + 1)].astype(BF)
        km_ref[0, hh, :, QK_NOPE:] = k_pe
        vm_ref[0, hh] = v[:, V_HEAD * hh:V_HEAD * (hh + 1)].astype(BF)


def _inproj(pos, h, weights, *, tm):
    b, t, d = h.shape
    nt = t // tm
    conv_ch = weights[1].shape[1]
    qk_w = GDN_HEADS * GDN_DK
    v_w = GDN_HEADS * GDN_DV
    hq = QK_NOPE + QK_ROPE

    def row(width):
        return pl.BlockSpec((1, tm, width), lambda bi, ti: (bi, ti, 0))

    def head(width):
        return pl.BlockSpec((1, MLA_HEADS, tm, width), lambda bi, ti: (bi, 0, ti, 0))

    out_shape = (
        jax.ShapeDtypeStruct((b, t, qk_w), BF), jax.ShapeDtypeStruct((b, t, qk_w), BF),
        jax.ShapeDtypeStruct((b, t, v_w), BF), jax.ShapeDtypeStruct((b, t, v_w), BF),
        jax.ShapeDtypeStruct((b, t, 2 * GDN_HEADS), F32),
        jax.ShapeDtypeStruct((b, MLA_HEADS, t, hq), BF), jax.ShapeDtypeStruct((b, MLA_HEADS, t, hq), BF),
        jax.ShapeDtypeStruct((b, MLA_HEADS, t, V_HEAD), BF),
        jax.ShapeDtypeStruct((b, t, d), BF), jax.ShapeDtypeStruct((b, t, d), BF),
    )
    out_specs = (row(qk_w), row(qk_w), row(v_w), row(v_w), row(2 * GDN_HEADS),
                 head(hq), head(hq), head(V_HEAD), row(d), row(d))
    return pl.pallas_call(
        functools.partial(_inproj_kernel, tm=tm),
        grid=(b, nt),
        in_specs=[pl.BlockSpec((1, 1, tm), lambda bi, ti: (bi, 0, ti)), row(d)]
        + [_resident(w.shape) for w in weights],
        out_specs=out_specs,
        out_shape=out_shape,
        scratch_shapes=[pltpu.VMEM((tm + CONV_HALO, conv_ch), F32)],
        compiler_params=pltpu.CompilerParams(
            dimension_semantics=("arbitrary", "arbitrary"), vmem_limit_bytes=56 * 1024 * 1024),
        name="inproj",
    )(pos, h, *weights)


def _gdn_kernel(q_ref, k_ref, v_ref, z_ref, bg_ref, nw_ref, o_ref, state_ref, *, tc):
    c = GDN_CHUNK

    @pl.when(pl.program_id(1) == 0)
    def _():
        state_ref[...] = jnp.zeros(state_ref.shape, F32)

    ii = lax.broadcasted_iota(jnp.int32, (c, c), 0)
    jj = lax.broadcasted_iota(jnp.int32, (c, c), 1)
    eye = (ii == jj).astype(F32)
    heads = range(GDN_HEADS)
    nchunk = tc // c
    probs = [(ci, hh) for ci in range(nchunk) for hh in heads]
    rows = [slice(ci * c, (ci + 1) * c) for ci, _ in probs]
    cols = [slice(LANES * hh, LANES * (hh + 1)) for _, hh in probs]
    ids = range(len(probs))
    bgs = [bg_ref[0, ci * c:(ci + 1) * c, :] for ci in range(nchunk)]
    q = [q_ref[0, rows[i], cols[i]] for i in ids]
    k = [k_ref[0, rows[i], cols[i]] for i in ids]
    v = [v_ref[0, rows[i], cols[i]] for i in ids]
    beta = [bgs[ci][:, hh:hh + 1] for ci, hh in probs]
    g = [bgs[ci][:, GDN_HEADS + hh:GDN_HEADS + hh + 1] for ci, hh in probs]
    gc_row = [jnp.sum(jnp.where(ii <= jj, x, 0.0), axis=0, keepdims=True) for x in g]
    gc_col = [jnp.sum(eye * x, axis=1, keepdims=True) for x in gc_row]
    g_last = [jnp.sum(x, axis=0, keepdims=True) for x in g]
    decay = [jnp.where(ii >= jj, jnp.exp(jnp.minimum(gc - gr, 0.0)), 0.0)
             for gc, gr in zip(gc_col, gc_row)]
    e_col = [jnp.exp(x) for x in gc_col]
    kf = [x.astype(F32) for x in k]
    kb = [x * b for x, b in zip(kf, beta)]
    both = [_dot_nt(jnp.concatenate([kb[i].astype(BF), q[i]], axis=0), k[i]) for i in ids]
    n = [jnp.where(ii > jj, -both[i][:c] * decay[i], 0.0) for i in ids]
    attn = [(both[i][c:] * decay[i]).astype(BF) for i in ids]
    a = [eye + x for x in n]
    nb = [x.astype(BF) for x in n]
    p = [_dot(x, x) for x in nb]
    for _ in range(4):
        pb = [x.astype(BF) for x in p]
        sq = [_dot(jnp.concatenate([pb[i], a[i].astype(BF)], axis=0), pb[i]) for i in ids]
        p = [x[:c] for x in sq]
        a = [a[i] + sq[i][c:] for i in ids]
    a = [a[i] + _dot(a[i].astype(BF), p[i].astype(BF)) for i in ids]
    rhs = [jnp.concatenate([v[i].astype(F32) * beta[i], kb[i] * e_col[i]], axis=1).astype(BF)
           for i in ids]
    sol = [_dot(a[i].astype(BF), rhs[i]) for i in ids]
    u = [x[:, :GDN_DV] for x in sol]
    wq = [jnp.concatenate([sol[i][:, GDN_DV:].astype(BF), (q[i].astype(F32) * e_col[i]).astype(BF)],
                          axis=0) for i in ids]
    k_dec = [(kf[i] * jnp.exp(g_last[i] - gc_col[i])).astype(BF) for i in ids]
    e_last = [jnp.exp(x) for x in g_last]
    state = [state_ref[hh] for hh in heads]
    for ci in range(nchunk):
        idx = [ci * GDN_HEADS + hh for hh in heads]
        ws = [_dot(wq[i], state[hh].astype(BF)) for hh, i in zip(heads, idx)]
        v_new = [(u[i] - ws[hh][:c]).astype(BF) for hh, i in zip(heads, idx)]
        o = [ws[hh][c:] + _dot(attn[i], v_new[hh]) for hh, i in zip(heads, idx)]
        state = [state[hh] * e_last[i] + _dot_tn(k_dec[i], v_new[hh]) for hh, i in zip(heads, idx)]
        for hh, i in zip(heads, idx):
            on = _rms(o[hh], nw_ref[...])
            o_ref[0, rows[i], cols[i]] = (on * z_ref[0, rows[i], cols[i]].astype(F32)).astype(BF)
    for hh in heads:
        state_ref[hh] = state[hh]


def _gdn(qg, kg, vg, z, bg, nw, *, tc):
    b, t, w = qg.shape
    row = pl.BlockSpec((1, tc, w), lambda bi, ti: (bi, ti, 0))
    return pl.pallas_call(
        functools.partial(_gdn_kernel, tc=tc),
        grid=(b, t // tc),
        in_specs=[row, row, row, row,
                  pl.BlockSpec((1, tc, bg.shape[2]), lambda bi, ti: (bi, ti, 0)),
                  _resident(nw.shape)],
        out_specs=row,
        out_shape=jax.ShapeDtypeStruct((b, t, w), BF),
        scratch_shapes=[pltpu.VMEM((GDN_HEADS, GDN_DK, GDN_DV), F32)],
        compiler_params=pltpu.CompilerParams(dimension_semantics=("arbitrary", "arbitrary")),
        name="gdn",
    )(qg, kg, vg, z, bg, nw)


def _attn_kernel(q_ref, k_ref, v_ref, o_ref, sa_ref, sb_ref, ma_ref, mb_ref, m_ref, l_ref, acc_ref,
                 *, tq):
    t = q_ref.shape[2]
    reps = tq // LANES
    ii = lax.broadcasted_iota(jnp.int32, (tq, tq), 0)
    jj = lax.broadcasted_iota(jnp.int32, (tq, tq), 1)

    def rows_of(kb):
        return pl.ds(pl.multiple_of(kb * tq, tq), tq)

    def scores(q, s_ref, mx_ref, kb, masked=False):
        s = _dot_nt(q, k_ref[0, 0, rows_of(kb), :])
        if masked:
            s = jnp.where(jj <= ii, s, NEG_BIG)
        s_ref[...] = s
        mx_ref[...] = jnp.broadcast_to(jnp.max(s, axis=-1, keepdims=True), mx_ref.shape)

    def update(s_ref, mx_ref, kb):
        m_prev = m_ref[...]
        m_next = jnp.maximum(m_prev, mx_ref[...])
        alpha = jnp.exp2(m_prev - m_next)
        m_ref[...] = m_next
        p = jnp.exp2(s_ref[...] - jnp.concatenate([m_next] * reps, axis=-1))
        part = p[:, 0:LANES]
        for w in range(1, reps):
            part = part + p[:, w * LANES:(w + 1) * LANES]
        l_ref[...] = alpha * l_ref[...] + part
        acc_ref[...] = alpha * acc_ref[...] + _dot(p.astype(BF), v_ref[0, 0, rows_of(kb), :])

    def q_body(qi, carry):
        q0 = pl.multiple_of(qi * tq, tq)
        q = q_ref[0, 0, pl.ds(q0, tq), :]
        m_ref[...] = jnp.full(m_ref.shape, NEG_BIG, F32)
        l_ref[...] = jnp.zeros(l_ref.shape, F32)
        acc_ref[...] = jnp.zeros(acc_ref.shape, F32)
        scores(q, sa_ref, ma_ref, qi, masked=True)

        def pair_body(pi, pend):
            scores(q, sb_ref, mb_ref, 2 * pi)
            update(sa_ref, ma_ref, pend)
            scores(q, sa_ref, ma_ref, 2 * pi + 1)
            update(sb_ref, mb_ref, 2 * pi)
            return 2 * pi + 1

        pend = lax.fori_loop(0, qi // 2, pair_body, qi)

        @pl.when(qi % 2 == 1)
        def _():
            scores(q, sb_ref, mb_ref, qi - 1)
            update(sa_ref, ma_ref, pend)
            update(sb_ref, mb_ref, qi - 1)

        @pl.when(qi % 2 == 0)
        def _():
            update(sa_ref, ma_ref, pend)

        l_tot = jnp.sum(l_ref[...], axis=-1, keepdims=True)
        o_ref[0, pl.ds(q0, tq), :] = (acc_ref[...] / l_tot).astype(BF)
        return carry

    lax.fori_loop(0, t // tq, q_body, 0)


def _attn(qm, km, vm, *, tq):
    b, h, t, hq = qm.shape
    dv = vm.shape[3]
    assert dv == LANES
    return pl.pallas_call(
        functools.partial(_attn_kernel, tq=tq),
        grid=(b, h),
        in_specs=[pl.BlockSpec((1, 1, t, hq), lambda bi, hi: (bi, hi, 0, 0)),
                  pl.BlockSpec((1, 1, t, hq), lambda bi, hi: (bi, hi, 0, 0)),
                  pl.BlockSpec((1, 1, t, dv), lambda bi, hi: (bi, hi, 0, 0))],
        out_specs=pl.BlockSpec((1, t, dv), lambda bi, hi: (bi, 0, hi)),
        out_shape=jax.ShapeDtypeStruct((b, t, h * dv), BF),
        scratch_shapes=[pltpu.VMEM((tq, tq), F32), pltpu.VMEM((tq, tq), F32),
                        pltpu.VMEM((tq, LANES), F32), pltpu.VMEM((tq, LANES), F32),
                        pltpu.VMEM((tq, LANES), F32), pltpu.VMEM((tq, LANES), F32),
                        pltpu.VMEM((tq, dv), F32)],
        compiler_params=pltpu.CompilerParams(dimension_semantics=("arbitrary", "arbitrary")),
        name="attn",
    )(qm, km, vm)


def _merge_kernel(h_ref, oa_ref, ob_ref, sa_ref, sb_ref, pa_ref, pb_ref, wo_ref, o_ref):
    ya = _dot(oa_ref[...], pa_ref[...])
    yb = _dot(ob_ref[...], pb_ref[...])
    merged = sa_ref[...].astype(F32) * ya + sb_ref[...].astype(F32) * yb
    o_ref[...] = h_ref[...] + _dot(merged.astype(BF), wo_ref[...])


def _merge(h, oa, ob, sa, sb, pa, pb, wo, *, tm):
    n, d = h.shape
    row = pl.BlockSpec((tm, d), lambda i: (i, 0))
    return pl.pallas_call(
        _merge_kernel,
        grid=(n // tm,),
        in_specs=[row, row, row, row, row, _resident(pa.shape), _resident(pb.shape),
                  _resident(wo.shape)],
        out_specs=row,
        out_shape=jax.ShapeDtypeStruct((n, d), F32),
        compiler_params=pltpu.CompilerParams(dimension_semantics=("arbitrary",)),
        name="merge",
    )(h, oa, ob, sa, sb, pa, pb, wo)


def _rot_cols(w):
    k, n = w.shape
    w4 = w.reshape(k, n // QK_ROPE, 2, QK_ROPE // 2)
    return jnp.concatenate([-w4[:, :, 1], w4[:, :, 0]], axis=-1).reshape(k, n)


def _mixer_weights(mix_norm, w_in, conv_w, a_log, dt_bias, q_a_norm, w_q_up, kv_a_norm, w_kv_up):
    d = w_in.shape[0]
    qk_w = GDN_HEADS * GDN_DK
    v_w = GDN_HEADS * GDN_DV
    widths = (qk_w, qk_w, v_w, v_w, GDN_HEADS, GDN_HEADS, Q_LORA, KV_LORA, QK_ROPE, d, d)
    offs = np.concatenate([[0], np.cumsum(widths)])
    seg = [w_in[:, int(offs[i]):int(offs[i + 1])] for i in range(len(widths))]
    wqkv = jnp.concatenate(seg[0:3], axis=1).astype(BF)
    wz = seg[3].astype(BF)
    wbg = jnp.concatenate(seg[4:6], axis=1).astype(BF)
    wqd = seg[6].astype(BF)
    wckv = seg[7].astype(BF)
    wkpe = jnp.concatenate([seg[8], _rot_cols(seg[8])], axis=1).astype(BF)
    wga = seg[9].astype(BF)
    wgb = seg[10].astype(BF)
    zeros = jnp.zeros((GDN_HEADS,), F32)
    gp = jnp.stack([jnp.concatenate([zeros, a_log]), jnp.concatenate([zeros, dt_bias])])
    wq = w_q_up.reshape(Q_LORA, MLA_HEADS, QK_NOPE + QK_ROPE)
    wqn = wq[:, :, :QK_NOPE].reshape(Q_LORA, MLA_HEADS * QK_NOPE).astype(BF)
    wqp_f = wq[:, :, QK_NOPE:].reshape(Q_LORA, MLA_HEADS * QK_ROPE)
    wqp = wqp_f.astype(BF)
    wqpr = _rot_cols(wqp_f).astype(BF)
    wkv = w_kv_up.reshape(KV_LORA, MLA_HEADS, QK_NOPE + V_HEAD)
    wkn = wkv[:, :, :QK_NOPE].reshape(KV_LORA, MLA_HEADS * QK_NOPE).astype(BF)
    wv = wkv[:, :, QK_NOPE:].reshape(KV_LORA, MLA_HEADS * V_HEAD).astype(BF)
    inv_freq = ROPE_THETA ** (-jnp.arange(0, QK_ROPE, 2, dtype=F32) / QK_ROPE)
    invf = jnp.tile(inv_freq, LANES // (QK_ROPE // 2))[None, :]
    return (mix_norm[None, :], wqkv, conv_w, wz, wbg, gp, wqd, wckv, wkpe, wga, wgb,
            q_a_norm[None, :], wqn, wqp, wqpr, kv_a_norm[None, :], wkn, wv, invf)


def kernel(x, positions, ffn1_norm, ffn1_w_gate, ffn1_w_up, ffn1_w_down, mix_norm, w_in, conv_w, a_log, dt_bias, gdn_norm, proj_a, q_a_norm, w_q_up, kv_a_norm, w_kv_up, proj_b, w_o, ffn2_norm, ffn2_w_gate, ffn2_w_up, ffn2_w_down, final_norm):
    b, t, d = x.shape
    n = b * t
    depth = ffn1_norm.shape[0]
    tm_ffn = min(512, n)
    tm_in = min(256, t)
    tc = min(256, t)
    tq = min(512, t)
    ones = jnp.ones((1, d), F32)
    pos = positions.reshape(b, 1, t)
    h = x.reshape(n, d)
    for l in range(depth):
        last = l == depth - 1
        h = _ffn(h, ffn1_norm[l][None, :], ffn1_w_gate[l].astype(BF), ffn1_w_up[l].astype(BF),
                 ffn1_w_down[l].astype(BF), ones, final=False, tm=tm_ffn)
        mw = _mixer_weights(mix_norm[l], w_in[l], conv_w[l], a_log[l], dt_bias[l], q_a_norm[l],
                            w_q_up[l], kv_a_norm[l], w_kv_up[l])
        qg, kg, vg, z, bg, qm, km, vm, sa, sb = _inproj(pos, h.reshape(b, t, d), mw, tm=tm_in)
        oa = _gdn(qg, kg, vg, z, bg, gdn_norm[l][None, :], tc=tc)
        ob = _attn(qm, km, vm, tq=tq)
        h = _merge(h, oa.reshape(n, d), ob.reshape(n, d), sa.reshape(n, d), sb.reshape(n, d),
                   proj_a[l].astype(BF), proj_b[l].astype(BF), w_o[l].astype(BF), tm=tm_ffn)
        h = _ffn(h, ffn2_norm[l][None, :], ffn2_w_gate[l].astype(BF), ffn2_w_up[l].astype(BF),
                 ffn2_w_down[l].astype(BF), final_norm[None, :] if last else ones,
                 final=last, tm=tm_ffn)
    return h.reshape(b, t, d)
```

```python
import functools

import numpy as np
import jax
import jax.numpy as jnp
from jax import lax
from jax.experimental import pallas as pl
from jax.experimental.pallas import tpu as pltpu

EPS = 1e-6
CONV_W = 4
GDN_HEADS = 8
GDN_DK = 128
GDN_DV = 128
GDN_CHUNK = 64
MLA_HEADS = 8
Q_LORA = 384
KV_LORA = 256
QK_NOPE = 128
QK_ROPE = 64
V_HEAD = 128
ROPE_THETA = 10000.0
LANES = 128
CONV_HALO = 8
assert CONV_HALO >= CONV_W - 1

BF = jnp.bfloat16
F32 = jnp.float32
NEG_BIG = -1e30


def _dot(a, b):
    return jnp.dot(a, b, preferred_element_type=F32)


def _dot_nt(a, b):
    return lax.dot_general(a, b, (((1,), (1,)), ((), ())), preferred_element_type=F32)


def _dot_tn(a, b):
    return lax.dot_general(a, b, (((0,), (0,)), ((), ())), preferred_element_type=F32)


def _rms(x, w):
    return x * lax.rsqrt(jnp.mean(x * x, axis=-1, keepdims=True) + EPS) * w


def _sigmoid(x):
    return 0.5 * jnp.tanh(0.5 * x) + 0.5


def _silu(x):
    h = 0.5 * x
    return h * jnp.tanh(h) + h


def _resident(shape):
    nd = len(shape)
    return pl.BlockSpec(shape, lambda *_: (0,) * nd, pipeline_mode=pl.Buffered(1))


def _ffn_kernel(x_ref, nw_ref, wg_ref, wu_ref, wd_ref, fw_ref, o_ref, *, final):
    x = x_ref[...]
    xn = _rms(x, nw_ref[...]).astype(BF)
    g = _dot(xn, wg_ref[...])
    u = _dot(xn, wu_ref[...])
    a = (_silu(g) * u).astype(BF)
    h = x + 0.5 * _dot(a, wd_ref[...])
    if final:
        h = _rms(h, fw_ref[...])
    o_ref[...] = h


def _ffn(x, nw, wg, wu, wd, fw, *, final, tm):
    n, d = x.shape
    ff = wg.shape[1]
    row = pl.BlockSpec((tm, d), lambda i: (i, 0))
    return pl.pallas_call(
        functools.partial(_ffn_kernel, final=final),
        grid=(n // tm,),
        in_specs=[row, _resident((1, d)), _resident((d, ff)), _resident((d, ff)),
                  _resident((ff, d)), _resident((1, d))],
        out_specs=row,
        out_shape=jax.ShapeDtypeStruct((n, d), F32),
        compiler_params=pltpu.CompilerParams(
            dimension_semantics=("arbitrary",), vmem_limit_bytes=56 * 1024 * 1024),
        name="ffn_final" if final else "ffn",
    )(x, nw, wg, wu, wd, fw)


def _inproj_kernel(pos_ref, h_ref, nw_ref, wqkv_ref, convw_ref, wz_ref, wbg_ref, gp_ref,
                   wqd_ref, wckv_ref, wkpe_ref, wga_ref, wgb_ref, qan_ref, wqn_ref, wqp_ref,
                   wqpr_ref, kvan_ref, wkn_ref, wv_ref, invf_ref,
                   qg_ref, kg_ref, vg_ref, z_ref, bg_ref, qm_ref, km_ref, vm_ref, sa_ref, sb_ref,
                   xbuf_ref, *, tm):
    u = _rms(h_ref[0], nw_ref[...]).astype(BF)

    @pl.when(pl.program_id(1) == 0)
    def _():
        xbuf_ref[0:CONV_HALO, :] = jnp.zeros((CONV_HALO, xbuf_ref.shape[1]), F32)

    xbuf_ref[CONV_HALO:CONV_HALO + tm, :] = _dot(u, wqkv_ref[...])
    n_slab = xbuf_ref.shape[1] // LANES
    per = n_slab // 3
    for s in range(n_slab):
        cols = slice(LANES * s, LANES * (s + 1))
        xs = xbuf_ref[:, cols]
        acc = convw_ref[CONV_W - 1:CONV_W, cols] * xs
        for j in range(CONV_W - 1):
            acc = acc + pltpu.roll(convw_ref[j:j + 1, cols] * xs, CONV_W - 1 - j, 0)
        y = _silu(acc[CONV_HALO:, :])
        if s < 2 * per:
            y = y * lax.rsqrt(jnp.sum(y * y, axis=-1, keepdims=True) + EPS)
        if s < per:
            y = y * (GDN_DK ** -0.5)
        dst = (qg_ref, kg_ref, vg_ref)[s // per]
        oc = slice(LANES * (s % per), LANES * (s % per + 1))
        dst[0, :, oc] = y.astype(BF)
    xbuf_ref[0:CONV_HALO, :] = xbuf_ref[tm:tm + CONV_HALO, :]

    zz = _dot(u, wz_ref[...])
    z_ref[0] = _silu(zz).astype(BF)

    ba = _dot(u, wbg_ref[...])
    sp_in = ba + gp_ref[1:2, :]
    softplus = jnp.maximum(sp_in, 0.0) + jnp.log1p(jnp.exp(-jnp.abs(sp_in)))
    lane = lax.broadcasted_iota(jnp.int32, ba.shape, 1)
    bg_ref[0] = jnp.where(lane < GDN_HEADS, _sigmoid(ba), -jnp.exp(gp_ref[0:1, :]) * softplus)

    sa_ref[0] = _sigmoid(_dot(u, wga_ref[...])).astype(BF)
    sb_ref[0] = _sigmoid(_dot(u, wgb_ref[...])).astype(BF)

    qn = _rms(_dot(u, wqd_ref[...]), qan_ref[...]).astype(BF)
    cn = _rms(_dot(u, wckv_ref[...]), kvan_ref[...]).astype(BF)
    pos_rows = jnp.broadcast_to(pos_ref[0].astype(F32), (LANES, tm))
    ang = pos_rows.T * invf_ref[...]
    cos = jnp.cos(ang)
    sin = jnp.sin(ang)
    reps = (MLA_HEADS * QK_ROPE) // LANES
    cos_h = jnp.concatenate([cos] * reps, axis=-1)
    sin_h = jnp.concatenate([sin] * reps, axis=-1)
    scale = (QK_NOPE + QK_ROPE) ** -0.5 * float(np.log2(np.e))
    q_nope = _dot(qn, wqn_ref[...]) * scale
    q_pe = (_dot(qn, wqp_ref[...]) * cos_h + _dot(qn, wqpr_ref[...]) * sin_h) * scale
    kp = _dot(u, wkpe_ref[...])
    k_pe = (kp[:, :QK_ROPE] * cos[:, :QK_ROPE] + kp[:, QK_ROPE:] * sin[:, :QK_ROPE]).astype(BF)
    k_nope = _dot(cn, wkn_ref[...])
    v = _dot(cn, wv_ref[...])
    for hh in range(MLA_HEADS):
        qm_ref[0, hh, :, 0:QK_NOPE] = q_nope[:, QK_NOPE * hh:QK_NOPE * (hh + 1)].astype(BF)
        qm_ref[0, hh, :, QK_NOPE:] = q_pe[:, QK_ROPE * hh:QK_ROPE * (hh + 1)].astype(BF)
        km_ref[0, hh, :, 0:QK_NOPE] = k_nope[:, QK_NOPE * hh:QK_NOPE * (hh + 1)].astype(BF)
        km_ref[0, hh, :, QK_NOPE:] = k_pe
        vm_ref[0, hh] = v[:, V_HEAD * hh:V_HEAD * (hh + 1)].astype(BF)


def _inproj(pos, h, weights, *, tm):
    b, t, d = h.shape
    nt = t // tm
    conv_ch = weights[1].shape[1]
    qk_w = GDN_HEADS * GDN_DK
    v_w = GDN_HEADS * GDN_DV
    hq = QK_NOPE + QK_ROPE

    def row(width):
        return pl.BlockSpec((1, tm, width), lambda bi, ti: (bi, ti, 0))

    def head(width):
        return pl.BlockSpec((1, MLA_HEADS, tm, width), lambda bi, ti: (bi, 0, ti, 0))

    out_shape = (
        jax.ShapeDtypeStruct((b, t, qk_w), BF), jax.ShapeDtypeStruct((b, t, qk_w), BF),
        jax.ShapeDtypeStruct((b, t, v_w), BF), jax.ShapeDtypeStruct((b, t, v_w), BF),
        jax.ShapeDtypeStruct((b, t, 2 * GDN_HEADS), F32),
        jax.ShapeDtypeStruct((b, MLA_HEADS, t, hq), BF), jax.ShapeDtypeStruct((b, MLA_HEADS, t, hq), BF),
        jax.ShapeDtypeStruct((b, MLA_HEADS, t, V_HEAD), BF),
        jax.ShapeDtypeStruct((b, t, d), BF), jax.ShapeDtypeStruct((b, t, d), BF),
    )
    out_specs = (row(qk_w), row(qk_w), row(v_w), row(v_w), row(2 * GDN_HEADS),
                 head(hq), head(hq), head(V_HEAD), row(d), row(d))
    return pl.pallas_call(
        functools.partial(_inproj_kernel, tm=tm),
        grid=(b, nt),
        in_specs=[pl.BlockSpec((1, 1, tm), lambda bi, ti: (bi, 0, ti)), row(d)]
        + [_resident(w.shape) for w in weights],
        out_specs=out_specs,
        out_shape=out_shape,
        scratch_shapes=[pltpu.VMEM((tm + CONV_HALO, conv_ch), F32)],
        compiler_params=pltpu.CompilerParams(
            dimension_semantics=("arbitrary", "arbitrary"), vmem_limit_bytes=56 * 1024 * 1024),
        name="inproj",
    )(pos, h, *weights)


def _gdn_kernel(q_ref, k_ref, v_ref, z_ref, bg_ref, nw_ref, o_ref, state_ref, *, tc):
    c = GDN_CHUNK

    @pl.when(pl.program_id(1) == 0)
    def _():
        state_ref[...] = jnp.zeros(state_ref.shape, F32)

    ii = lax.broadcasted_iota(jnp.int32, (c, c), 0)
    jj = lax.broadcasted_iota(jnp.int32, (c, c), 1)
    eye = (ii == jj).astype(F32)
    heads = range(GDN_HEADS)
    nchunk = tc // c
    probs = [(ci, hh) for ci in range(nchunk) for hh in heads]
    rows = [slice(ci * c, (ci + 1) * c) for ci, _ in probs]
    cols = [slice(LANES * hh, LANES * (hh + 1)) for _, hh in probs]
    ids = range(len(probs))
    bgs = [bg_ref[0, ci * c:(ci + 1) * c, :] for ci in range(nchunk)]
    q = [q_ref[0, rows[i], cols[i]] for i in ids]
    k = [k_ref[0, rows[i], cols[i]] for i in ids]
    v = [v_ref[0, rows[i], cols[i]] for i in ids]
    beta = [bgs[ci][:, hh:hh + 1] for ci, hh in probs]
    g = [bgs[ci][:, GDN_HEADS + hh:GDN_HEADS + hh + 1] for ci, hh in probs]
    gc_row = [jnp.sum(jnp.where(ii <= jj, x, 0.0), axis=0, keepdims=True) for x in g]
    gc_col = [jnp.sum(eye * x, axis=1, keepdims=True) for x in gc_row]
    g_last = [jnp.sum(x, axis=0, keepdims=True) for x in g]
    decay = [jnp.where(ii >= jj, jnp.exp(jnp.minimum(gc - gr, 0.0)), 0.0)
             for gc, gr in zip(gc_col, gc_row)]
    e_col = [jnp.exp(x) for x in gc_col]
    kf = [x.astype(F32) for x in k]
    kb = [x * b for x, b in zip(kf, beta)]
    both = [_dot_nt(jnp.concatenate([kb[i].astype(BF), q[i]], axis=0), k[i]) for i in ids]
    n = [jnp.where(ii > jj, -both[i][:c] * decay[i], 0.0) for i in ids]
    attn = [(both[i][c:] * decay[i]).astype(BF) for i in ids]
    a = [eye + x for x in n]
    nb = [x.astype(BF) for x in n]
    p = [_dot(x, x) for x in nb]
    for _ in range(4):
        pb = [x.astype(BF) for x in p]
        sq = [_dot(jnp.concatenate([pb[i], a[i].astype(BF)], axis=0), pb[i]) for i in ids]
        p = [x[:c] for x in sq]
        a = [a[i] + sq[i][c:] for i in ids]
    a = [a[i] + _dot(a[i].astype(BF), p[i].astype(BF)) for i in ids]
    rhs = [jnp.concatenate([v[i].astype(F32) * beta[i], kb[i] * e_col[i]], axis=1).astype(BF)
           for i in ids]
    sol = [_dot(a[i].astype(BF), rhs[i]) for i in ids]
    u = [x[:, :GDN_DV] for x in sol]
    wq = [jnp.concatenate([sol[i][:, GDN_DV:].astype(BF), (q[i].astype(F32) * e_col[i]).astype(BF)],
                          axis=0) for i in ids]
    k_dec = [(kf[i] * jnp.exp(g_last[i] - gc_col[i])).astype(BF) for i in ids]
    e_last = [jnp.exp(x) for x in g_last]
    state = [state_ref[hh] for hh in heads]
    for ci in range(nchunk):
        idx = [ci * GDN_HEADS + hh for hh in heads]
        ws = [_dot(wq[i], state[hh].astype(BF)) for hh, i in zip(heads, idx)]
        v_new = [(u[i] - ws[hh][:c]).astype(BF) for hh, i in zip(heads, idx)]
        o = [ws[hh][c:] + _dot(attn[i], v_new[hh]) for hh, i in zip(heads, idx)]
        state = [state[hh] * e_last[i] + _dot_tn(k_dec[i], v_new[hh]) for hh, i in zip(heads, idx)]
        for hh, i in zip(heads, idx):
            on = _rms(o[hh], nw_ref[...])
            o_ref[0, rows[i], cols[i]] = (on * z_ref[0, rows[i], cols[i]].astype(F32)).astype(BF)
    for hh in heads:
        state_ref[hh] = state[hh]


def _gdn(qg, kg, vg, z, bg, nw, *, tc):
    b, t, w = qg.shape
    row = pl.BlockSpec((1, tc, w), lambda bi, ti: (bi, ti, 0))
    return pl.pallas_call(
        functools.partial(_gdn_kernel, tc=tc),
        grid=(b, t // tc),
        in_specs=[row, row, row, row,
                  pl.BlockSpec((1, tc, bg.shape[2]), lambda bi, ti: (bi, ti, 0)),
                  _resident(nw.shape)],
        out_specs=row,
        out_shape=jax.ShapeDtypeStruct((b, t, w), BF),
        scratch_shapes=[pltpu.VMEM((GDN_HEADS, GDN_DK, GDN_DV), F32)],
        compiler_params=pltpu.CompilerParams(dimension_semantics=("arbitrary", "arbitrary")),
        name="gdn",
    )(qg, kg, vg, z, bg, nw)


def _attn_kernel(q_ref, k_ref, v_ref, o_ref, sa_ref, sb_ref, ma_ref, mb_ref, m_ref, l_ref, acc_ref,
                 *, tq):
    t = q_ref.shape[2]
    reps = tq // LANES
    ii = lax.broadcasted_iota(jnp.int32, (tq, tq), 0)
    jj = lax.broadcasted_iota(jnp.int32, (tq, tq), 1)
    bufs = ((sa_ref, ma_ref), (sb_ref, mb_ref))

    def rows_of(blk):
        return slice(blk * tq, (blk + 1) * tq)

    def scores(step, buf):
        qi, kb = step
        s_ref, mx_ref = buf
        s = _dot_nt(q_ref[0, 0, rows_of(qi), :], k_ref[0, 0, rows_of(kb), :])
        if kb == qi:
            s = jnp.where(jj <= ii, s, NEG_BIG)
        s_ref[...] = s
        mx_ref[...] = jnp.broadcast_to(jnp.max(s, axis=-1, keepdims=True), mx_ref.shape)

    def update(step, buf):
        qi, kb = step
        s_ref, mx_ref = buf
        first = kb == qi
        last = kb == qi - 1 or qi == 0
        if first:
            m_next = mx_ref[...]
        else:
            m_prev = m_ref[...]
            m_next = jnp.maximum(m_prev, mx_ref[...])
            alpha = jnp.exp2(m_prev - m_next)
        m_ref[...] = m_next
        p = jnp.exp2(s_ref[...] - jnp.concatenate([m_next] * reps, axis=-1))
        part = p[:, 0:LANES]
        for w in range(1, reps):
            part = part + p[:, w * LANES:(w + 1) * LANES]
        pv = _dot(p.astype(BF), v_ref[0, 0, rows_of(kb), :])
        if first:
            l_ref[...] = part
            acc_ref[...] = pv
        else:
            l_ref[...] = alpha * l_ref[...] + part
            acc_ref[...] = alpha * acc_ref[...] + pv
        if last:
            l_tot = jnp.sum(l_ref[...], axis=-1, keepdims=True)
            o_ref[0, rows_of(qi), :] = (acc_ref[...] / l_tot).astype(BF)

    steps = [(qi, kb) for qi in range(t // tq) for kb in [qi] + list(range(qi))]
    scores(steps[0], bufs[0])
    for si, step in enumerate(steps):
        if si + 1 < len(steps):
            scores(steps[si + 1], bufs[(si + 1) % 2])
        update(step, bufs[si % 2])


def _attn(qm, km, vm, *, tq):
    b, h, t, hq = qm.shape
    dv = vm.shape[3]
    assert dv == LANES
    return pl.pallas_call(
        functools.partial(_attn_kernel, tq=tq),
        grid=(b, h),
        in_specs=[pl.BlockSpec((1, 1, t, hq), lambda bi, hi: (bi, hi, 0, 0)),
                  pl.BlockSpec((1, 1, t, hq), lambda bi, hi: (bi, hi, 0, 0)),
                  pl.BlockSpec((1, 1, t, dv), lambda bi, hi: (bi, hi, 0, 0))],
        out_specs=pl.BlockSpec((1, t, dv), lambda bi, hi: (bi, 0, hi)),
        out_shape=jax.ShapeDtypeStruct((b, t, h * dv), BF),
        scratch_shapes=[pltpu.VMEM((tq, tq), F32), pltpu.VMEM((tq, tq), F32),
                        pltpu.VMEM((tq, LANES), F32), pltpu.VMEM((tq, LANES), F32),
                        pltpu.VMEM((tq, LANES), F32), pltpu.VMEM((tq, LANES), F32),
                        pltpu.VMEM((tq, dv), F32)],
        compiler_params=pltpu.CompilerParams(dimension_semantics=("arbitrary", "arbitrary")),
        name="attn",
    )(qm, km, vm)


def _merge_kernel(h_ref, oa_ref, ob_ref, sa_ref, sb_ref, pa_ref, pb_ref, wo_ref, o_ref):
    ya = _dot(oa_ref[...], pa_ref[...])
    yb = _dot(ob_ref[...], pb_ref[...])
    merged = sa_ref[...].astype(F32) * ya + sb_ref[...].astype(F32) * yb
    o_ref[...] = h_ref[...] + _dot(merged.astype(BF), wo_ref[...])


def _merge(h, oa, ob, sa, sb, pa, pb, wo, *, tm):
    n, d = h.shape
    row = pl.BlockSpec((tm, d), lambda i: (i, 0))
    return pl.pallas_call(
        _merge_kernel,
        grid=(n // tm,),
        in_specs=[row, row, row, row, row, _resident(pa.shape), _resident(pb.shape),
                  _resident(wo.shape)],
        out_specs=row,
        out_shape=jax.ShapeDtypeStruct((n, d), F32),
        compiler_params=pltpu.CompilerParams(dimension_semantics=("arbitrary",)),
        name="merge",
    )(h, oa, ob, sa, sb, pa, pb, wo)


def _rot_cols(w):
    k, n = w.shape
    w4 = w.reshape(k, n // QK_ROPE, 2, QK_ROPE // 2)
    return jnp.concatenate([-w4[:, :, 1], w4[:, :, 0]], axis=-1).reshape(k, n)


def _mixer_weights(mix_norm, w_in, conv_w, a_log, dt_bias, q_a_norm, w_q_up, kv_a_norm, w_kv_up):
    d = w_in.shape[0]
    qk_w = GDN_HEADS * GDN_DK
    v_w = GDN_HEADS * GDN_DV
    widths = (qk_w, qk_w, v_w, v_w, GDN_HEADS, GDN_HEADS, Q_LORA, KV_LORA, QK_ROPE, d, d)
    offs = np.concatenate([[0], np.cumsum(widths)])
    w_in = w_in.astype(BF)
    seg = [w_in[:, int(offs[i]):int(offs[i + 1])] for i in range(len(widths))]
    wqkv = jnp.concatenate(seg[0:3], axis=1)
    wz = seg[3]
    wbg = jnp.concatenate(seg[4:6], axis=1)
    wqd = seg[6]
    wckv = seg[7]
    wkpe = jnp.concatenate([seg[8], _rot_cols(seg[8])], axis=1)
    wga = seg[9]
    wgb = seg[10]
    zeros = jnp.zeros((GDN_HEADS,), F32)
    gp = jnp.stack([jnp.concatenate([zeros, a_log]), jnp.concatenate([zeros, dt_bias])])
    wq = w_q_up.reshape(Q_LORA, MLA_HEADS, QK_NOPE + QK_ROPE)
    wqn = wq[:, :, :QK_NOPE].reshape(Q_LORA, MLA_HEADS * QK_NOPE).astype(BF)
    wqp_f = wq[:, :, QK_NOPE:].reshape(Q_LORA, MLA_HEADS * QK_ROPE)
    wqp = wqp_f.astype(BF)
    wqpr = _rot_cols(wqp_f).astype(BF)
    wkv = w_kv_up.reshape(KV_LORA, MLA_HEADS, QK_NOPE + V_HEAD)
    wkn = wkv[:, :, :QK_NOPE].reshape(KV_LORA, MLA_HEADS * QK_NOPE).astype(BF)
    wv = wkv[:, :, QK_NOPE:].reshape(KV_LORA, MLA_HEADS * V_HEAD).astype(BF)
    inv_freq = ROPE_THETA ** (-jnp.arange(0, QK_ROPE, 2, dtype=F32) / QK_ROPE)
    invf = jnp.tile(inv_freq, LANES // (QK_ROPE // 2))[None, :]
    return (mix_norm[None, :], wqkv, conv_w, wz, wbg, gp, wqd, wckv, wkpe, wga, wgb,
            q_a_norm[None, :], wqn, wqp, wqpr, kv_a_norm[None, :], wkn, wv, invf)


def kernel(x, positions, ffn1_norm, ffn1_w_gate, ffn1_w_up, ffn1_w_down, mix_norm, w_in, conv_w, a_log, dt_bias, gdn_norm, proj_a, q_a_norm, w_q_up, kv_a_norm, w_kv_up, proj_b, w_o, ffn2_norm, ffn2_w_gate, ffn2_w_up, ffn2_w_down, final_norm):
    b, t, d = x.shape
    n = b * t
    depth = ffn1_norm.shape[0]
    tm_ffn = min(512, n)
    tm_in = min(256, t)
    tc = min(256, t)
    tq = min(512, t)
    ones = jnp.ones((1, d), F32)
    pos = positions.reshape(b, 1, t)
    h = x.reshape(n, d)
    for l in range(depth):
        last = l == depth - 1
        h = _ffn(h, ffn1_norm[l][None, :], ffn1_w_gate[l].astype(BF), ffn1_w_up[l].astype(BF),
                 ffn1_w_down[l].astype(BF), ones, final=False, tm=tm_ffn)
        mw = _mixer_weights(mix_norm[l], w_in[l], conv_w[l], a_log[l], dt_bias[l], q_a_norm[l],
                            w_q_up[l], kv_a_norm[l], w_kv_up[l])
        qg, kg, vg, z, bg, qm, km, vm, sa, sb = _inproj(pos, h.reshape(b, t, d), mw, tm=tm_in)
        oa = _gdn(qg, kg, vg, z, bg, gdn_norm[l][None, :], tc=tc)
        ob = _attn(qm, km, vm, tq=tq)
        h = _merge(h, oa.reshape(n, d), ob.reshape(n, d), sa.reshape(n, d), sb.reshape(n, d),
                   proj_a[l].astype(BF), proj_b[l].astype(BF), w_o[l].astype(BF), tm=tm_ffn)
        h = _ffn(h, ffn2_norm[l][None, :], ffn2_w_gate[l].astype(BF), ffn2_w_up[l].astype(BF),
                 ffn2_w_down[l].astype(BF), final_norm[None, :] if last else ones,
                 final=last, tm=tm_ffn)
    return h.reshape(b, t, d)
```

```python
import functools

import numpy as np
import jax
import jax.numpy as jnp
from jax import lax
from jax.experimental import pallas as pl
from jax.experimental.pallas import tpu as pltpu

EPS = 1e-6
CONV_W = 4
GDN_HEADS = 8
GDN_DK = 128
GDN_DV = 128
GDN_CHUNK = 64
GDN_GROUP = 2
MLA_HEADS = 8
Q_LORA = 384
KV_LORA = 256
QK_NOPE = 128
QK_ROPE = 64
V_HEAD = 128
ROPE_THETA = 10000.0
LANES = 128
CONV_HALO = 8
assert CONV_HALO >= CONV_W - 1

BF = jnp.bfloat16
F32 = jnp.float32
NEG_BIG = -1e30


def _dot(a, b):
    return jnp.dot(a, b, preferred_element_type=F32)


def _dot_nt(a, b):
    return lax.dot_general(a, b, (((1,), (1,)), ((), ())), preferred_element_type=F32)


def _dot_tn(a, b):
    return lax.dot_general(a, b, (((0,), (0,)), ((), ())), preferred_element_type=F32)


def _rms(x, w):
    return x * lax.rsqrt(jnp.mean(x * x, axis=-1, keepdims=True) + EPS) * w


def _sigmoid(x):
    return 0.5 * jnp.tanh(0.5 * x) + 0.5


def _silu(x):
    h = 0.5 * x
    return h * jnp.tanh(h) + h


def _resident(shape):
    nd = len(shape)
    return pl.BlockSpec(shape, lambda *_: (0,) * nd, pipeline_mode=pl.Buffered(1))


def _ffn_kernel(x_ref, nw_ref, wg_ref, wu_ref, wd_ref, fw_ref, o_ref, *, final):
    x = x_ref[...]
    xn = _rms(x, nw_ref[...]).astype(BF)
    g = _dot(xn, wg_ref[...])
    u = _dot(xn, wu_ref[...])
    a = (_silu(g) * u).astype(BF)
    h = x + 0.5 * _dot(a, wd_ref[...])
    if final:
        h = _rms(h, fw_ref[...])
    o_ref[...] = h


def _ffn(x, nw, wg, wu, wd, fw, *, final, tm):
    n, d = x.shape
    ff = wg.shape[1]
    row = pl.BlockSpec((tm, d), lambda i: (i, 0))
    return pl.pallas_call(
        functools.partial(_ffn_kernel, final=final),
        grid=(n // tm,),
        in_specs=[row, _resident((1, d)), _resident((d, ff)), _resident((d, ff)),
                  _resident((ff, d)), _resident((1, d))],
        out_specs=row,
        out_shape=jax.ShapeDtypeStruct((n, d), F32),
        compiler_params=pltpu.CompilerParams(
            dimension_semantics=("arbitrary",), vmem_limit_bytes=56 * 1024 * 1024),
        name="ffn_final" if final else "ffn",
    )(x, nw, wg, wu, wd, fw)


def _inproj_kernel(pos_ref, h_ref, nw_ref, wqkv_ref, convw_ref, wz_ref, wbg_ref, gp_ref,
                   wqd_ref, wckv_ref, wkpe_ref, wga_ref, wgb_ref, qan_ref, wqn_ref, wqp_ref,
                   wqpr_ref, kvan_ref, wkn_ref, wv_ref, invf_ref,
                   qg_ref, kg_ref, vg_ref, z_ref, bg_ref, qm_ref, km_ref, vm_ref, sa_ref, sb_ref,
                   xbuf_ref, *, tm):
    u = _rms(h_ref[0], nw_ref[...]).astype(BF)

    @pl.when(pl.program_id(1) == 0)
    def _():
        xbuf_ref[0:CONV_HALO, :] = jnp.zeros((CONV_HALO, xbuf_ref.shape[1]), F32)

    xbuf_ref[CONV_HALO:CONV_HALO + tm, :] = _dot(u, wqkv_ref[...])
    n_slab = xbuf_ref.shape[1] // LANES
    per = n_slab // 3
    for s in range(n_slab):
        cols = slice(LANES * s, LANES * (s + 1))
        xs = xbuf_ref[:, cols]
        acc = convw_ref[CONV_W - 1:CONV_W, cols] * xs
        for j in range(CONV_W - 1):
            acc = acc + pltpu.roll(convw_ref[j:j + 1, cols] * xs, CONV_W - 1 - j, 0)
        y = _silu(acc[CONV_HALO:, :])
        if s < 2 * per:
            y = y * lax.rsqrt(jnp.sum(y * y, axis=-1, keepdims=True) + EPS)
        if s < per:
            y = y * (GDN_DK ** -0.5)
        dst = (qg_ref, kg_ref, vg_ref)[s // per]
        oc = slice(LANES * (s % per), LANES * (s % per + 1))
        dst[0, :, oc] = y.astype(BF)
    xbuf_ref[0:CONV_HALO, :] = xbuf_ref[tm:tm + CONV_HALO, :]

    zz = _dot(u, wz_ref[...])
    z_ref[0] = _silu(zz).astype(BF)

    ba = _dot(u, wbg_ref[...])
    sp_in = ba + gp_ref[1:2, :]
    softplus = jnp.maximum(sp_in, 0.0) + jnp.log1p(jnp.exp(-jnp.abs(sp_in)))
    lane = lax.broadcasted_iota(jnp.int32, ba.shape, 1)
    bg_ref[0] = jnp.where(lane < GDN_HEADS, _sigmoid(ba), -jnp.exp(gp_ref[0:1, :]) * softplus)

    sa_ref[0] = _sigmoid(_dot(u, wga_ref[...])).astype(BF)
    sb_ref[0] = _sigmoid(_dot(u, wgb_ref[...])).astype(BF)

    qn = _rms(_dot(u, wqd_ref[...]), qan_ref[...]).astype(BF)
    cn = _rms(_dot(u, wckv_ref[...]), kvan_ref[...]).astype(BF)
    pos_rows = jnp.broadcast_to(pos_ref[0].astype(F32), (LANES, tm))
    ang = pos_rows.T * invf_ref[...]
    cos = jnp.cos(ang)
    sin = jnp.sin(ang)
    reps = (MLA_HEADS * QK_ROPE) // LANES
    cos_h = jnp.concatenate([cos] * reps, axis=-1)
    sin_h = jnp.concatenate([sin] * reps, axis=-1)
    scale = (QK_NOPE + QK_ROPE) ** -0.5 * float(np.log2(np.e))
    q_nope = _dot(qn, wqn_ref[...]) * scale
    q_pe = (_dot(qn, wqp_ref[...]) * cos_h + _dot(qn, wqpr_ref[...]) * sin_h) * scale
    kp = _dot(u, wkpe_ref[...])
    k_pe = (kp[:, :QK_ROPE] * cos[:, :QK_ROPE] + kp[:, QK_ROPE:] * sin[:, :QK_ROPE]).astype(BF)
    k_nope = _dot(cn, wkn_ref[...])
    v = _dot(cn, wv_ref[...])
    for hh in range(MLA_HEADS):
        qm_ref[0, hh, :, 0:QK_NOPE] = q_nope[:, QK_NOPE * hh:QK_NOPE * (hh + 1)].astype(BF)
        qm_ref[0, hh, :, QK_NOPE:] = q_pe[:, QK_ROPE * hh:QK_ROPE * (hh + 1)].astype(BF)
        km_ref[0, hh, :, 0:QK_NOPE] = k_nope[:, QK_NOPE * hh:QK_NOPE * (hh + 1)].astype(BF)
        km_ref[0, hh, :, QK_NOPE:] = k_pe
        vm_ref[0, hh] = v[:, V_HEAD * hh:V_HEAD * (hh + 1)].astype(BF)


def _inproj(pos, h, weights, *, tm):
    b, t, d = h.shape
    nt = t // tm
    conv_ch = weights[1].shape[1]
    qk_w = GDN_HEADS * GDN_DK
    v_w = GDN_HEADS * GDN_DV
    hq = QK_NOPE + QK_ROPE

    def row(width):
        return pl.BlockSpec((1, tm, width), lambda bi, ti: (bi, ti, 0))

    def head(width):
        return pl.BlockSpec((1, MLA_HEADS, tm, width), lambda bi, ti: (bi, 0, ti, 0))

    out_shape = (
        jax.ShapeDtypeStruct((b, t, qk_w), BF), jax.ShapeDtypeStruct((b, t, qk_w), BF),
        jax.ShapeDtypeStruct((b, t, v_w), BF), jax.ShapeDtypeStruct((b, t, v_w), BF),
        jax.ShapeDtypeStruct((b, t, 2 * GDN_HEADS), F32),
        jax.ShapeDtypeStruct((b, MLA_HEADS, t, hq), BF), jax.ShapeDtypeStruct((b, MLA_HEADS, t, hq), BF),
        jax.ShapeDtypeStruct((b, MLA_HEADS, t, V_HEAD), BF),
        jax.ShapeDtypeStruct((b, t, d), BF), jax.ShapeDtypeStruct((b, t, d), BF),
    )
    out_specs = (row(qk_w), row(qk_w), row(v_w), row(v_w), row(2 * GDN_HEADS),
                 head(hq), head(hq), head(V_HEAD), row(d), row(d))
    return pl.pallas_call(
        functools.partial(_inproj_kernel, tm=tm),
        grid=(b, nt),
        in_specs=[pl.BlockSpec((1, 1, tm), lambda bi, ti: (bi, 0, ti)), row(d)]
        + [_resident(w.shape) for w in weights],
        out_specs=out_specs,
        out_shape=out_shape,
        scratch_shapes=[pltpu.VMEM((tm + CONV_HALO, conv_ch), F32)],
        compiler_params=pltpu.CompilerParams(
            dimension_semantics=("arbitrary", "arbitrary"), vmem_limit_bytes=56 * 1024 * 1024),
        name="inproj",
    )(pos, h, *weights)


def _gdn_kernel(q_ref, k_ref, v_ref, z_ref, bg_ref, nw_ref, o_ref, state_ref, *, tc, gsz):
    c = GDN_CHUNK

    @pl.when(pl.program_id(1) == 0)
    def _():
        state_ref[...] = jnp.zeros(state_ref.shape, F32)

    ii = lax.broadcasted_iota(jnp.int32, (c, c), 0)
    jj = lax.broadcasted_iota(jnp.int32, (c, c), 1)
    eye = (ii == jj).astype(F32)
    heads = range(GDN_HEADS)
    nchunk = tc // c

    def local(chunks, out):
        probs = [(ci, hh) for ci in chunks for hh in heads]
        rows = [slice(ci * c, (ci + 1) * c) for ci, _ in probs]
        cols = [slice(LANES * hh, LANES * (hh + 1)) for _, hh in probs]
        ids = range(len(probs))
        bgs = {ci: bg_ref[0, ci * c:(ci + 1) * c, :] for ci in chunks}
        q = [q_ref[0, rows[i], cols[i]] for i in ids]
        k = [k_ref[0, rows[i], cols[i]] for i in ids]
        v = [v_ref[0, rows[i], cols[i]] for i in ids]
        beta = [bgs[ci][:, hh:hh + 1] for ci, hh in probs]
        g = [bgs[ci][:, GDN_HEADS + hh:GDN_HEADS + hh + 1] for ci, hh in probs]
        gc_row = [jnp.sum(jnp.where(ii <= jj, x, 0.0), axis=0, keepdims=True) for x in g]
        gc_col = [jnp.sum(eye * x, axis=1, keepdims=True) for x in gc_row]
        g_last = [jnp.sum(x, axis=0, keepdims=True) for x in g]
        decay = [jnp.where(ii >= jj, jnp.exp(jnp.minimum(gc - gr, 0.0)), 0.0)
                 for gc, gr in zip(gc_col, gc_row)]
        e_col = [jnp.exp(x) for x in gc_col]
        kf = [x.astype(F32) for x in k]
        kb = [x * b for x, b in zip(kf, beta)]
        yield
        both = [_dot_nt(jnp.concatenate([kb[i].astype(BF), q[i]], axis=0), k[i]) for i in ids]
        n = [jnp.where(ii > jj, -both[i][:c] * decay[i], 0.0) for i in ids]
        attn = [(both[i][c:] * decay[i]).astype(BF) for i in ids]
        yield
        a = [eye + x for x in n]
        nb = [x.astype(BF) for x in n]
        p = [_dot(x, x) for x in nb]
        yield
        for _ in range(4):
            pb = [x.astype(BF) for x in p]
            sq = [_dot(jnp.concatenate([pb[i], a[i].astype(BF)], axis=0), pb[i]) for i in ids]
            p = [x[:c] for x in sq]
            a = [a[i] + sq[i][c:] for i in ids]
            yield
        a = [a[i] + _dot(a[i].astype(BF), p[i].astype(BF)) for i in ids]
        rhs = [jnp.concatenate([v[i].astype(F32) * beta[i], kb[i] * e_col[i]], axis=1).astype(BF)
               for i in ids]
        yield
        sol = [_dot(a[i].astype(BF), rhs[i]) for i in ids]
        for i, (ci, hh) in enumerate(probs):
            out[ci, hh] = dict(
                rows=rows[i], cols=cols[i], u=sol[i][:, :GDN_DV], attn=attn[i],
                wq=jnp.concatenate([sol[i][:, GDN_DV:].astype(BF),
                                    (q[i].astype(F32) * e_col[i]).astype(BF)], axis=0),
                k_dec=(kf[i] * jnp.exp(g_last[i] - gc_col[i])).astype(BF),
                e_last=jnp.exp(g_last[i]))
        yield

    def recurrence(chunks, loc, state):
        for ci in chunks:
            pr = [loc[ci, hh] for hh in heads]
            ws = [_dot(pr[hh]["wq"], state[hh].astype(BF)) for hh in heads]
            v_new = [(pr[hh]["u"] - ws[hh][:c]).astype(BF) for hh in heads]
            yield
            o = [ws[hh][c:] + _dot(pr[hh]["attn"], v_new[hh]) for hh in heads]
            for hh in heads:
                state[hh] = state[hh] * pr[hh]["e_last"] + _dot_tn(pr[hh]["k_dec"], v_new[hh])
            for hh in heads:
                on = _rms(o[hh], nw_ref[...])
                rows, cols = pr[hh]["rows"], pr[hh]["cols"]
                o_ref[0, rows, cols] = (on * z_ref[0, rows, cols].astype(F32)).astype(BF)
            yield

    state = [state_ref[hh] for hh in heads]
    groups = [list(range(gi, min(gi + gsz, nchunk))) for gi in range(0, nchunk, gsz)]
    loc = {}
    rec = iter(())
    for grp in groups:
        for _ in local(grp, loc):
            next(rec, None)
        for _ in rec:
            pass
        rec = recurrence(grp, loc, state)
    for _ in rec:
        pass
    for hh in heads:
        state_ref[hh] = state[hh]


def _gdn(qg, kg, vg, z, bg, nw, *, tc, gsz):
    b, t, w = qg.shape
    row = pl.BlockSpec((1, tc, w), lambda bi, ti: (bi, ti, 0))
    return pl.pallas_call(
        functools.partial(_gdn_kernel, tc=tc, gsz=gsz),
        grid=(b, t // tc),
        in_specs=[row, row, row, row,
                  pl.BlockSpec((1, tc, bg.shape[2]), lambda bi, ti: (bi, ti, 0)),
                  _resident(nw.shape)],
        out_specs=row,
        out_shape=jax.ShapeDtypeStruct((b, t, w), BF),
        scratch_shapes=[pltpu.VMEM((GDN_HEADS, GDN_DK, GDN_DV), F32)],
        compiler_params=pltpu.CompilerParams(dimension_semantics=("arbitrary", "arbitrary")),
        name="gdn",
    )(qg, kg, vg, z, bg, nw)


def _attn_kernel(q_ref, k_ref, v_ref, o_ref, sa_ref, sb_ref, ma_ref, mb_ref, m_ref, l_ref, acc_ref,
                 *, tq):
    t = q_ref.shape[2]
    reps = tq // LANES
    ii = lax.broadcasted_iota(jnp.int32, (tq, tq), 0)
    jj = lax.broadcasted_iota(jnp.int32, (tq, tq), 1)
    bufs = ((sa_ref, ma_ref), (sb_ref, mb_ref))

    def rows_of(blk):
        return slice(blk * tq, (blk + 1) * tq)

    def scores(step, buf):
        qi, kb = step
        s_ref, mx_ref = buf
        s = _dot_nt(q_ref[0, 0, rows_of(qi), :], k_ref[0, 0, rows_of(kb), :])
        if kb == qi:
            s = jnp.where(jj <= ii, s, NEG_BIG)
        s_ref[...] = s
        mx_ref[...] = jnp.broadcast_to(jnp.max(s, axis=-1, keepdims=True), mx_ref.shape)

    def update(step, buf):
        qi, kb = step
        s_ref, mx_ref = buf
        first = kb == qi
        last = kb == qi - 1 or qi == 0
        if first:
            m_next = mx_ref[...]
        else:
            m_prev = m_ref[...]
            m_next = jnp.maximum(m_prev, mx_ref[...])
            alpha = jnp.exp2(m_prev - m_next)
        m_ref[...] = m_next
        p = jnp.exp2(s_ref[...] - jnp.concatenate([m_next] * reps, axis=-1))
        part = p[:, 0:LANES]
        for w in range(1, reps):
            part = part + p[:, w * LANES:(w + 1) * LANES]
        pv = _dot(p.astype(BF), v_ref[0, 0, rows_of(kb), :])
        if first:
            l_ref[...] = part
            acc_ref[...] = pv
        else:
            l_ref[...] = alpha * l_ref[...] + part
            acc_ref[...] = alpha * acc_ref[...] + pv
        if last:
            l_tot = jnp.sum(l_ref[...], axis=-1, keepdims=True)
            o_ref[0, rows_of(qi), :] = (acc_ref[...] / l_tot).astype(BF)

    steps = [(qi, kb) for qi in range(t // tq) for kb in [qi] + list(range(qi))]
    scores(steps[0], bufs[0])
    for si, step in enumerate(steps):
        if si + 1 < len(steps):
            scores(steps[si + 1], bufs[(si + 1) % 2])
        update(step, bufs[si % 2])


def _attn(qm, km, vm, *, tq):
    b, h, t, hq = qm.shape
    dv = vm.shape[3]
    assert dv == LANES
    return pl.pallas_call(
        functools.partial(_attn_kernel, tq=tq),
        grid=(b, h),
        in_specs=[pl.BlockSpec((1, 1, t, hq), lambda bi, hi: (bi, hi, 0, 0)),
                  pl.BlockSpec((1, 1, t, hq), lambda bi, hi: (bi, hi, 0, 0)),
                  pl.BlockSpec((1, 1, t, dv), lambda bi, hi: (bi, hi, 0, 0))],
        out_specs=pl.BlockSpec((1, t, dv), lambda bi, hi: (bi, 0, hi)),
        out_shape=jax.ShapeDtypeStruct((b, t, h * dv), BF),
        scratch_shapes=[pltpu.VMEM((tq, tq), F32), pltpu.VMEM((tq, tq), F32),
                        pltpu.VMEM((tq, LANES), F32), pltpu.VMEM((tq, LANES), F32),
                        pltpu.VMEM((tq, LANES), F32), pltpu.VMEM((tq, LANES), F32),
                        pltpu.VMEM((tq, dv), F32)],
        compiler_params=pltpu.CompilerParams(dimension_semantics=("arbitrary", "arbitrary")),
        name="attn",
    )(qm, km, vm)


def _merge_kernel(h_ref, oa_ref, ob_ref, sa_ref, sb_ref, pa_ref, pb_ref, wo_ref, o_ref):
    ya = _dot(oa_ref[...], pa_ref[...])
    yb = _dot(ob_ref[...], pb_ref[...])
    merged = sa_ref[...].astype(F32) * ya + sb_ref[...].astype(F32) * yb
    o_ref[...] = h_ref[...] + _dot(merged.astype(BF), wo_ref[...])


def _merge(h, oa, ob, sa, sb, pa, pb, wo, *, tm):
    n, d = h.shape
    row = pl.BlockSpec((tm, d), lambda i: (i, 0))
    return pl.pallas_call(
        _merge_kernel,
        grid=(n // tm,),
        in_specs=[row, row, row, row, row, _resident(pa.shape), _resident(pb.shape),
                  _resident(wo.shape)],
        out_specs=row,
        out_shape=jax.ShapeDtypeStruct((n, d), F32),
        compiler_params=pltpu.CompilerParams(dimension_semantics=("arbitrary",)),
        name="merge",
    )(h, oa, ob, sa, sb, pa, pb, wo)


def _rot_cols(w):
    k, n = w.shape
    w4 = w.reshape(k, n // QK_ROPE, 2, QK_ROPE // 2)
    return jnp.concatenate([-w4[:, :, 1], w4[:, :, 0]], axis=-1).reshape(k, n)


def _mixer_weights(mix_norm, w_in, conv_w, a_log, dt_bias, q_a_norm, w_q_up, kv_a_norm, w_kv_up):
    d = w_in.shape[0]
    qk_w = GDN_HEADS * GDN_DK
    v_w = GDN_HEADS * GDN_DV
    widths = (qk_w, qk_w, v_w, v_w, GDN_HEADS, GDN_HEADS, Q_LORA, KV_LORA, QK_ROPE, d, d)
    offs = np.concatenate([[0], np.cumsum(widths)])
    w_in = w_in.astype(BF)
    seg = [w_in[:, int(offs[i]):int(offs[i + 1])] for i in range(len(widths))]
    wqkv = jnp.concatenate(seg[0:3], axis=1)
    wz = seg[3]
    wbg = jnp.concatenate(seg[4:6], axis=1)
    wqd = seg[6]
    wckv = seg[7]
    wkpe = jnp.concatenate([seg[8], _rot_cols(seg[8])], axis=1)
    wga = seg[9]
    wgb = seg[10]
    zeros = jnp.zeros((GDN_HEADS,), F32)
    gp = jnp.stack([jnp.concatenate([zeros, a_log]), jnp.concatenate([zeros, dt_bias])])
    wq = w_q_up.reshape(Q_LORA, MLA_HEADS, QK_NOPE + QK_ROPE)
    wqn = wq[:, :, :QK_NOPE].reshape(Q_LORA, MLA_HEADS * QK_NOPE).astype(BF)
    wqp_f = wq[:, :, QK_NOPE:].reshape(Q_LORA, MLA_HEADS * QK_ROPE)
    wqp = wqp_f.astype(BF)
    wqpr = _rot_cols(wqp_f).astype(BF)
    wkv = w_kv_up.reshape(KV_LORA, MLA_HEADS, QK_NOPE + V_HEAD)
    wkn = wkv[:, :, :QK_NOPE].reshape(KV_LORA, MLA_HEADS * QK_NOPE).astype(BF)
    wv = wkv[:, :, QK_NOPE:].reshape(KV_LORA, MLA_HEADS * V_HEAD).astype(BF)
    inv_freq = ROPE_THETA ** (-jnp.arange(0, QK_ROPE, 2, dtype=F32) / QK_ROPE)
    invf = jnp.tile(inv_freq, LANES // (QK_ROPE // 2))[None, :]
    return (mix_norm[None, :], wqkv, conv_w, wz, wbg, gp, wqd, wckv, wkpe, wga, wgb,
            q_a_norm[None, :], wqn, wqp, wqpr, kv_a_norm[None, :], wkn, wv, invf)


def kernel(x, positions, ffn1_norm, ffn1_w_gate, ffn1_w_up, ffn1_w_down, mix_norm, w_in, conv_w, a_log, dt_bias, gdn_norm, proj_a, q_a_norm, w_q_up, kv_a_norm, w_kv_up, proj_b, w_o, ffn2_norm, ffn2_w_gate, ffn2_w_up, ffn2_w_down, final_norm):
    b, t, d = x.shape
    n = b * t
    depth = ffn1_norm.shape[0]
    tm_ffn = min(512, n)
    tm_in = min(256, t)
    tc = min(1024, t)
    tq = min(512, t)
    ones = jnp.ones((1, d), F32)
    pos = positions.reshape(b, 1, t)
    h = x.reshape(n, d)
    for l in range(depth):
        last = l == depth - 1
        h = _ffn(h, ffn1_norm[l][None, :], ffn1_w_gate[l].astype(BF), ffn1_w_up[l].astype(BF),
                 ffn1_w_down[l].astype(BF), ones, final=False, tm=tm_ffn)
        mw = _mixer_weights(mix_norm[l], w_in[l], conv_w[l], a_log[l], dt_bias[l], q_a_norm[l],
                            w_q_up[l], kv_a_norm[l], w_kv_up[l])
        qg, kg, vg, z, bg, qm, km, vm, sa, sb = _inproj(pos, h.reshape(b, t, d), mw, tm=tm_in)
        oa = _gdn(qg, kg, vg, z, bg, gdn_norm[l][None, :], tc=tc, gsz=GDN_GROUP)
        ob = _attn(qm, km, vm, tq=tq)
        h = _merge(h, oa.reshape(n, d), ob.reshape(n, d), sa.reshape(n, d), sb.reshape(n, d),
                   proj_a[l].astype(BF), proj_b[l].astype(BF), w_o[l].astype(BF), tm=tm_ffn)
        h = _ffn(h, ffn2_norm[l][None, :], ffn2_w_gate[l].astype(BF), ffn2_w_up[l].astype(BF),
                 ffn2_w_down[l].astype(BF), final_norm[None, :] if last else ones,
                 final=last, tm=tm_ffn)
    return h.reshape(b, t, d)
```

```python
import functools

import numpy as np
import jax
import jax.numpy as jnp
from jax import lax
from jax.experimental import pallas as pl
from jax.experimental.pallas import tpu as pltpu

EPS = 1e-6
CONV_W = 4
GDN_HEADS = 8
GDN_DK = 128
GDN_DV = 128
GDN_CHUNK = 64
GDN_GROUP = 2
MLA_HEADS = 8
Q_LORA = 384
KV_LORA = 256
QK_NOPE = 128
QK_ROPE = 64
V_HEAD = 128
ROPE_THETA = 10000.0
LANES = 128
CONV_HALO = 8
assert CONV_HALO >= CONV_W - 1
VMEM_LIMIT_BYTES = 56 * 1024 * 1024
FFN_ROWS = 512
INPROJ_ROWS = 256
GDN_ROWS = 1024
ATTN_ROWS = 512

BF = jnp.bfloat16
F32 = jnp.float32
NEG_BIG = -1e30


def _dot(a, b):
    return jnp.dot(a, b, preferred_element_type=F32)


def _dot_nt(a, b):
    return lax.dot_general(a, b, (((1,), (1,)), ((), ())), preferred_element_type=F32)


def _dot_tn(a, b):
    return lax.dot_general(a, b, (((0,), (0,)), ((), ())), preferred_element_type=F32)


def _rms(x, w):
    return x * lax.rsqrt(jnp.mean(x * x, axis=-1, keepdims=True) + EPS) * w


def _sigmoid(x):
    return 0.5 * jnp.tanh(0.5 * x) + 0.5


def _silu(x):
    h = 0.5 * x
    return h * jnp.tanh(h) + h


def _resident(shape):
    nd = len(shape)
    return pl.BlockSpec(shape, lambda *_: (0,) * nd, pipeline_mode=pl.Buffered(1))


def _ffn_kernel(x_ref, nw_ref, wg_ref, wu_ref, wd_ref, fw_ref, o_ref, *, final):
    x = x_ref[...]
    xn = _rms(x, nw_ref[...]).astype(BF)
    g = _dot(xn, wg_ref[...])
    u = _dot(xn, wu_ref[...])
    a = (_silu(g) * u).astype(BF)
    h = x + 0.5 * _dot(a, wd_ref[...])
    if final:
        h = _rms(h, fw_ref[...])
    o_ref[...] = h


def _ffn(x, nw, wg, wu, wd, fw, *, final, tm):
    n, d = x.shape
    ff = wg.shape[1]
    row = pl.BlockSpec((tm, d), lambda i: (i, 0))
    return pl.pallas_call(
        functools.partial(_ffn_kernel, final=final),
        grid=(n // tm,),
        in_specs=[row, _resident((1, d)), _resident((d, ff)), _resident((d, ff)),
                  _resident((ff, d)), _resident((1, d))],
        out_specs=row,
        out_shape=jax.ShapeDtypeStruct((n, d), F32),
        compiler_params=pltpu.CompilerParams(
            dimension_semantics=("arbitrary",), vmem_limit_bytes=VMEM_LIMIT_BYTES),
        name="ffn_final" if final else "ffn",
    )(x, nw, wg, wu, wd, fw)


def _inproj_kernel(pos_ref, h_ref, nw_ref, wqkv_ref, convw_ref, wz_ref, wbg_ref, gp_ref,
                   wqd_ref, wckv_ref, wkpe_ref, wga_ref, wgb_ref, qan_ref, wqn_ref, wqp_ref,
                   wqpr_ref, kvan_ref, wkn_ref, wv_ref, invf_ref,
                   qg_ref, kg_ref, vg_ref, z_ref, bg_ref, qm_ref, km_ref, vm_ref, sa_ref, sb_ref,
                   xbuf_ref, *, tm):
    u = _rms(h_ref[0], nw_ref[...]).astype(BF)

    @pl.when(pl.program_id(1) == 0)
    def _():
        xbuf_ref[0:CONV_HALO, :] = jnp.zeros((CONV_HALO, xbuf_ref.shape[1]), F32)

    xbuf_ref[CONV_HALO:CONV_HALO + tm, :] = _dot(u, wqkv_ref[...])
    n_slab = xbuf_ref.shape[1] // LANES
    per = n_slab // 3
    for s in range(n_slab):
        cols = slice(LANES * s, LANES * (s + 1))
        xs = xbuf_ref[:, cols]
        acc = convw_ref[CONV_W - 1:CONV_W, cols] * xs
        for j in range(CONV_W - 1):
            acc = acc + pltpu.roll(convw_ref[j:j + 1, cols] * xs, CONV_W - 1 - j, 0)
        y = _silu(acc[CONV_HALO:, :])
        if s < 2 * per:
            inv = lax.rsqrt(jnp.sum(y * y, axis=-1, keepdims=True) + EPS)
            y = y * (inv * (GDN_DK ** -0.5) if s < per else inv)
        dst = (qg_ref, kg_ref, vg_ref)[s // per]
        oc = slice(LANES * (s % per), LANES * (s % per + 1))
        dst[0, :, oc] = y.astype(BF)
    xbuf_ref[0:CONV_HALO, :] = xbuf_ref[tm:tm + CONV_HALO, :]

    zz = _dot(u, wz_ref[...])
    z_ref[0] = _silu(zz).astype(BF)

    ba = _dot(u, wbg_ref[...])
    sp_in = ba + gp_ref[1:2, :]
    softplus = jnp.maximum(sp_in, 0.0) + jnp.log1p(jnp.exp(-jnp.abs(sp_in)))
    lane = lax.broadcasted_iota(jnp.int32, ba.shape, 1)
    bg_ref[0] = jnp.where(lane < GDN_HEADS, _sigmoid(ba), -jnp.exp(gp_ref[0:1, :]) * softplus)

    sa_ref[0] = _sigmoid(_dot(u, wga_ref[...])).astype(BF)
    sb_ref[0] = _sigmoid(_dot(u, wgb_ref[...])).astype(BF)

    qn = _rms(_dot(u, wqd_ref[...]), qan_ref[...]).astype(BF)
    cn = _rms(_dot(u, wckv_ref[...]), kvan_ref[...]).astype(BF)
    pos_rows = jnp.broadcast_to(pos_ref[0].astype(F32), (LANES, tm))
    ang = pos_rows.T * invf_ref[...]
    cos = jnp.cos(ang)
    sin = jnp.sin(ang)
    reps = (MLA_HEADS * QK_ROPE) // LANES
    cos_h = jnp.concatenate([cos] * reps, axis=-1)
    sin_h = jnp.concatenate([sin] * reps, axis=-1)
    scale = (QK_NOPE + QK_ROPE) ** -0.5 * float(np.log2(np.e))
    q_nope = _dot(qn, wqn_ref[...]) * scale
    q_pe = (_dot(qn, wqp_ref[...]) * cos_h + _dot(qn, wqpr_ref[...]) * sin_h) * scale
    kp = _dot(u, wkpe_ref[...])
    k_pe = (kp[:, :QK_ROPE] * cos[:, :QK_ROPE] + kp[:, QK_ROPE:] * sin[:, :QK_ROPE]).astype(BF)
    k_nope = _dot(cn, wkn_ref[...])
    v = _dot(cn, wv_ref[...])
    for hh in range(MLA_HEADS):
        qm_ref[0, hh, :, 0:QK_NOPE] = q_nope[:, QK_NOPE * hh:QK_NOPE * (hh + 1)].astype(BF)
        qm_ref[0, hh, :, QK_NOPE:] = q_pe[:, QK_ROPE * hh:QK_ROPE * (hh + 1)].astype(BF)
        km_ref[0, hh, :, 0:QK_NOPE] = k_nope[:, QK_NOPE * hh:QK_NOPE * (hh + 1)].astype(BF)
        km_ref[0, hh, :, QK_NOPE:] = k_pe
        vm_ref[0, hh] = v[:, V_HEAD * hh:V_HEAD * (hh + 1)].astype(BF)


def _inproj(pos, h, weights, *, tm):
    b, t, d = h.shape
    nt = t // tm
    conv_ch = weights[1].shape[1]
    qk_w = GDN_HEADS * GDN_DK
    v_w = GDN_HEADS * GDN_DV
    hq = QK_NOPE + QK_ROPE

    def row(width):
        return pl.BlockSpec((1, tm, width), lambda bi, ti: (bi, ti, 0))

    def head(width):
        return pl.BlockSpec((1, MLA_HEADS, tm, width), lambda bi, ti: (bi, 0, ti, 0))

    out_shape = (
        jax.ShapeDtypeStruct((b, t, qk_w), BF), jax.ShapeDtypeStruct((b, t, qk_w), BF),
        jax.ShapeDtypeStruct((b, t, v_w), BF), jax.ShapeDtypeStruct((b, t, v_w), BF),
        jax.ShapeDtypeStruct((b, t, 2 * GDN_HEADS), F32),
        jax.ShapeDtypeStruct((b, MLA_HEADS, t, hq), BF), jax.ShapeDtypeStruct((b, MLA_HEADS, t, hq), BF),
        jax.ShapeDtypeStruct((b, MLA_HEADS, t, V_HEAD), BF),
        jax.ShapeDtypeStruct((b, t, d), BF), jax.ShapeDtypeStruct((b, t, d), BF),
    )
    out_specs = (row(qk_w), row(qk_w), row(v_w), row(v_w), row(2 * GDN_HEADS),
                 head(hq), head(hq), head(V_HEAD), row(d), row(d))
    return pl.pallas_call(
        functools.partial(_inproj_kernel, tm=tm),
        grid=(b, nt),
        in_specs=[pl.BlockSpec((1, 1, tm), lambda bi, ti: (bi, 0, ti)), row(d)]
        + [_resident(w.shape) for w in weights],
        out_specs=out_specs,
        out_shape=out_shape,
        scratch_shapes=[pltpu.VMEM((tm + CONV_HALO, conv_ch), F32)],
        compiler_params=pltpu.CompilerParams(
            dimension_semantics=("arbitrary", "arbitrary"), vmem_limit_bytes=VMEM_LIMIT_BYTES),
        name="inproj",
    )(pos, h, *weights)


def _gdn_kernel(q_ref, k_ref, v_ref, z_ref, bg_ref, nw_ref, o_ref, state_ref, *, tc, gsz):
    c = GDN_CHUNK

    @pl.when(pl.program_id(1) == 0)
    def _():
        state_ref[...] = jnp.zeros(state_ref.shape, F32)

    ii = lax.broadcasted_iota(jnp.int32, (c, c), 0)
    jj = lax.broadcasted_iota(jnp.int32, (c, c), 1)
    eye = (ii == jj).astype(F32)
    heads = range(GDN_HEADS)
    nchunk = tc // c

    def local(chunks, out):
        probs = [(ci, hh) for ci in chunks for hh in heads]
        rows = [slice(ci * c, (ci + 1) * c) for ci, _ in probs]
        cols = [slice(LANES * hh, LANES * (hh + 1)) for _, hh in probs]
        ids = range(len(probs))
        bgs = {ci: bg_ref[0, ci * c:(ci + 1) * c, :] for ci in chunks}
        q = [q_ref[0, rows[i], cols[i]] for i in ids]
        k = [k_ref[0, rows[i], cols[i]] for i in ids]
        v = [v_ref[0, rows[i], cols[i]] for i in ids]
        beta = [bgs[ci][:, hh:hh + 1] for ci, hh in probs]
        g = [bgs[ci][:, GDN_HEADS + hh:GDN_HEADS + hh + 1] for ci, hh in probs]
        gc_row = [jnp.sum(jnp.where(ii <= jj, x, 0.0), axis=0, keepdims=True) for x in g]
        gc_col = [jnp.sum(eye * x, axis=1, keepdims=True) for x in gc_row]
        g_last = [jnp.sum(x, axis=0, keepdims=True) for x in g]
        decay = [jnp.where(ii >= jj, jnp.exp(jnp.minimum(gc - gr, 0.0)), 0.0)
                 for gc, gr in zip(gc_col, gc_row)]
        e_col = [jnp.exp(x) for x in gc_col]
        kf = [x.astype(F32) for x in k]
        kb = [x * b for x, b in zip(kf, beta)]
        yield
        both = [_dot_nt(jnp.concatenate([kb[i].astype(BF), q[i]], axis=0), k[i]) for i in ids]
        n = [jnp.where(ii > jj, -both[i][:c] * decay[i], 0.0) for i in ids]
        attn = [(both[i][c:] * decay[i]).astype(BF) for i in ids]
        yield
        a = [eye + x for x in n]
        nb = [x.astype(BF) for x in n]
        p = [_dot(x, x) for x in nb]
        yield
        for _ in range(4):
            pb = [x.astype(BF) for x in p]
            sq = [_dot(jnp.concatenate([pb[i], a[i].astype(BF)], axis=0), pb[i]) for i in ids]
            p = [x[:c] for x in sq]
            a = [a[i] + sq[i][c:] for i in ids]
            yield
        a = [a[i] + _dot(a[i].astype(BF), p[i].astype(BF)) for i in ids]
        rhs = [jnp.concatenate([v[i].astype(F32) * beta[i], kb[i] * e_col[i]], axis=1).astype(BF)
               for i in ids]
        yield
        sol = [_dot(a[i].astype(BF), rhs[i]) for i in ids]
        for i, (ci, hh) in enumerate(probs):
            out[ci, hh] = dict(
                rows=rows[i], cols=cols[i], u=sol[i][:, :GDN_DV], attn=attn[i],
                wq=jnp.concatenate([sol[i][:, GDN_DV:].astype(BF),
                                    (q[i].astype(F32) * e_col[i]).astype(BF)], axis=0),
                k_dec=(kf[i] * jnp.exp(g_last[i] - gc_col[i])).astype(BF),
                e_last=jnp.exp(g_last[i]))
        yield

    def recurrence(chunks, loc, state):
        for ci in chunks:
            pr = [loc[ci, hh] for hh in heads]
            ws = [_dot(pr[hh]["wq"], state[hh].astype(BF)) for hh in heads]
            v_new = [(pr[hh]["u"] - ws[hh][:c]).astype(BF) for hh in heads]
            yield
            o = [ws[hh][c:] + _dot(pr[hh]["attn"], v_new[hh]) for hh in heads]
            for hh in heads:
                state[hh] = state[hh] * pr[hh]["e_last"] + _dot_tn(pr[hh]["k_dec"], v_new[hh])
            for hh in heads:
                on = _rms(o[hh], nw_ref[...])
                rows, cols = pr[hh]["rows"], pr[hh]["cols"]
                o_ref[0, rows, cols] = (on * z_ref[0, rows, cols].astype(F32)).astype(BF)
            yield

    state = [state_ref[hh] for hh in heads]
    groups = [list(range(gi, min(gi + gsz, nchunk))) for gi in range(0, nchunk, gsz)]
    loc = {}
    rec = iter(())
    for grp in groups:
        for _ in local(grp, loc):
            next(rec, None)
        for _ in rec:
            pass
        rec = recurrence(grp, loc, state)
    for _ in rec:
        pass
    for hh in heads:
        state_ref[hh] = state[hh]


def _gdn(qg, kg, vg, z, bg, nw, *, tc, gsz):
    b, t, w = qg.shape
    row = pl.BlockSpec((1, tc, w), lambda bi, ti: (bi, ti, 0))
    return pl.pallas_call(
        functools.partial(_gdn_kernel, tc=tc, gsz=gsz),
        grid=(b, t // tc),
        in_specs=[row, row, row, row,
                  pl.BlockSpec((1, tc, bg.shape[2]), lambda bi, ti: (bi, ti, 0)),
                  _resident(nw.shape)],
        out_specs=row,
        out_shape=jax.ShapeDtypeStruct((b, t, w), BF),
        scratch_shapes=[pltpu.VMEM((GDN_HEADS, GDN_DK, GDN_DV), F32)],
        compiler_params=pltpu.CompilerParams(dimension_semantics=("arbitrary", "arbitrary")),
        name="gdn",
    )(qg, kg, vg, z, bg, nw)


def _attn_kernel(q_ref, k_ref, v_ref, o_ref, sa_ref, sb_ref, ma_ref, mb_ref, m_ref, l_ref, acc_ref,
                 *, tq):
    t = q_ref.shape[2]
    reps = tq // LANES
    ii = lax.broadcasted_iota(jnp.int32, (tq, tq), 0)
    jj = lax.broadcasted_iota(jnp.int32, (tq, tq), 1)
    bufs = ((sa_ref, ma_ref), (sb_ref, mb_ref))

    def rows_of(blk):
        return slice(blk * tq, (blk + 1) * tq)

    def scores(step, buf):
        qi, kb = step
        s_ref, mx_ref = buf
        s = _dot_nt(q_ref[0, 0, rows_of(qi), :], k_ref[0, 0, rows_of(kb), :])
        if kb == qi:
            s = jnp.where(jj <= ii, s, NEG_BIG)
        s_ref[...] = s
        mx_ref[...] = jnp.broadcast_to(jnp.max(s, axis=-1, keepdims=True), mx_ref.shape)

    def update(step, buf):
        qi, kb = step
        s_ref, mx_ref = buf
        first = kb == qi
        last = kb == qi - 1 or qi == 0
        if first:
            m_next = mx_ref[...]
        else:
            m_prev = m_ref[...]
            m_next = jnp.maximum(m_prev, mx_ref[...])
            alpha = jnp.exp2(m_prev - m_next)
        m_ref[...] = m_next
        p = jnp.exp2(s_ref[...] - jnp.concatenate([m_next] * reps, axis=-1))
        part = p[:, 0:LANES]
        for w in range(1, reps):
            part = part + p[:, w * LANES:(w + 1) * LANES]
        pv = _dot(p.astype(BF), v_ref[0, 0, rows_of(kb), :])
        if first:
            l_ref[...] = part
            acc_ref[...] = pv
        else:
            l_ref[...] = alpha * l_ref[...] + part
            acc_ref[...] = alpha * acc_ref[...] + pv
        if last:
            l_tot = jnp.sum(l_ref[...], axis=-1, keepdims=True)
            o_ref[0, rows_of(qi), :] = (acc_ref[...] / l_tot).astype(BF)

    steps = [(qi, kb) for qi in range(t // tq) for kb in [qi] + list(range(qi))]
    scores(steps[0], bufs[0])
    for si, step in enumerate(steps):
        if si + 1 < len(steps):
            scores(steps[si + 1], bufs[(si + 1) % 2])
        update(step, bufs[si % 2])


def _attn(qm, km, vm, *, tq):
    b, h, t, hq = qm.shape
    dv = vm.shape[3]
    assert dv == LANES
    return pl.pallas_call(
        functools.partial(_attn_kernel, tq=tq),
        grid=(b, h),
        in_specs=[pl.BlockSpec((1, 1, t, hq), lambda bi, hi: (bi, hi, 0, 0)),
                  pl.BlockSpec((1, 1, t, hq), lambda bi, hi: (bi, hi, 0, 0)),
                  pl.BlockSpec((1, 1, t, dv), lambda bi, hi: (bi, hi, 0, 0))],
        out_specs=pl.BlockSpec((1, t, dv), lambda bi, hi: (bi, 0, hi)),
        out_shape=jax.ShapeDtypeStruct((b, t, h * dv), BF),
        scratch_shapes=[pltpu.VMEM((tq, tq), F32), pltpu.VMEM((tq, tq), F32),
                        pltpu.VMEM((tq, LANES), F32), pltpu.VMEM((tq, LANES), F32),
                        pltpu.VMEM((tq, LANES), F32), pltpu.VMEM((tq, LANES), F32),
                        pltpu.VMEM((tq, dv), F32)],
        compiler_params=pltpu.CompilerParams(dimension_semantics=("arbitrary", "arbitrary")),
        name="attn",
    )(qm, km, vm)


def _merge_kernel(h_ref, oa_ref, ob_ref, sa_ref, sb_ref, pa_ref, pb_ref, wo_ref, o_ref):
    ya = _dot(oa_ref[...], pa_ref[...])
    yb = _dot(ob_ref[...], pb_ref[...])
    merged = sa_ref[...].astype(F32) * ya + sb_ref[...].astype(F32) * yb
    o_ref[...] = h_ref[...] + _dot(merged.astype(BF), wo_ref[...])


def _merge(h, oa, ob, sa, sb, pa, pb, wo, *, tm):
    n, d = h.shape
    row = pl.BlockSpec((tm, d), lambda i: (i, 0))
    return pl.pallas_call(
        _merge_kernel,
        grid=(n // tm,),
        in_specs=[row, row, row, row, row, _resident(pa.shape), _resident(pb.shape),
                  _resident(wo.shape)],
        out_specs=row,
        out_shape=jax.ShapeDtypeStruct((n, d), F32),
        compiler_params=pltpu.CompilerParams(dimension_semantics=("arbitrary",)),
        name="merge",
    )(h, oa, ob, sa, sb, pa, pb, wo)


def _rot_cols(w):
    k, n = w.shape
    w4 = w.reshape(k, n // QK_ROPE, 2, QK_ROPE // 2)
    return jnp.concatenate([-w4[:, :, 1], w4[:, :, 0]], axis=-1).reshape(k, n)


def _mixer_weights(mix_norm, w_in, conv_w, a_log, dt_bias, q_a_norm, w_q_up, kv_a_norm, w_kv_up):
    d = w_in.shape[0]
    qk_w = GDN_HEADS * GDN_DK
    v_w = GDN_HEADS * GDN_DV
    widths = (qk_w, qk_w, v_w, v_w, GDN_HEADS, GDN_HEADS, Q_LORA, KV_LORA, QK_ROPE, d, d)
    offs = np.concatenate([[0], np.cumsum(widths)])
    w_in = w_in.astype(BF)
    seg = [w_in[:, int(offs[i]):int(offs[i + 1])] for i in range(len(widths))]
    wqkv = jnp.concatenate(seg[0:3], axis=1)
    wz = seg[3]
    wbg = jnp.concatenate(seg[4:6], axis=1)
    wqd = seg[6]
    wckv = seg[7]
    wkpe = jnp.concatenate([seg[8], _rot_cols(seg[8])], axis=1)
    wga = seg[9]
    wgb = seg[10]
    zeros = jnp.zeros((GDN_HEADS,), F32)
    gp = jnp.stack([jnp.concatenate([zeros, a_log]), jnp.concatenate([zeros, dt_bias])])
    wq = w_q_up.reshape(Q_LORA, MLA_HEADS, QK_NOPE + QK_ROPE)
    wqn = wq[:, :, :QK_NOPE].reshape(Q_LORA, MLA_HEADS * QK_NOPE).astype(BF)
    wqp_f = wq[:, :, QK_NOPE:].reshape(Q_LORA, MLA_HEADS * QK_ROPE)
    wqp = wqp_f.astype(BF)
    wqpr = _rot_cols(wqp_f).astype(BF)
    wkv = w_kv_up.reshape(KV_LORA, MLA_HEADS, QK_NOPE + V_HEAD)
    wkn = wkv[:, :, :QK_NOPE].reshape(KV_LORA, MLA_HEADS * QK_NOPE).astype(BF)
    wv = wkv[:, :, QK_NOPE:].reshape(KV_LORA, MLA_HEADS * V_HEAD).astype(BF)
    inv_freq = ROPE_THETA ** (-jnp.arange(0, QK_ROPE, 2, dtype=F32) / QK_ROPE)
    invf = jnp.tile(inv_freq, LANES // (QK_ROPE // 2))[None, :]
    return (mix_norm[None, :], wqkv, conv_w, wz, wbg, gp, wqd, wckv, wkpe, wga, wgb,
            q_a_norm[None, :], wqn, wqp, wqpr, kv_a_norm[None, :], wkn, wv, invf)


def kernel(x, positions, ffn1_norm, ffn1_w_gate, ffn1_w_up, ffn1_w_down, mix_norm, w_in, conv_w, a_log, dt_bias, gdn_norm, proj_a, q_a_norm, w_q_up, kv_a_norm, w_kv_up, proj_b, w_o, ffn2_norm, ffn2_w_gate, ffn2_w_up, ffn2_w_down, final_norm):
    b, t, d = x.shape
    n = b * t
    depth = ffn1_norm.shape[0]
    tm_ffn = min(FFN_ROWS, n)
    tm_in = min(INPROJ_ROWS, t)
    tc = min(GDN_ROWS, t)
    tq = min(ATTN_ROWS, t)
    assert n % tm_ffn == 0 and t % tm_in == 0 and t % tc == 0 and t % tq == 0
    assert tc % (GDN_CHUNK * GDN_GROUP) == 0 and tq % LANES == 0
    ones = jnp.ones((1, d), F32)
    pos = positions.reshape(b, 1, t)
    h = x.reshape(n, d)
    for l in range(depth):
        last = l == depth - 1
        h = _ffn(h, ffn1_norm[l][None, :], ffn1_w_gate[l].astype(BF), ffn1_w_up[l].astype(BF),
                 ffn1_w_down[l].astype(BF), ones, final=False, tm=tm_ffn)
        mw = _mixer_weights(mix_norm[l], w_in[l], conv_w[l], a_log[l], dt_bias[l], q_a_norm[l],
                            w_q_up[l], kv_a_norm[l], w_kv_up[l])
        qg, kg, vg, z, bg, qm, km, vm, sa, sb = _inproj(pos, h.reshape(b, t, d), mw, tm=tm_in)
        oa = _gdn(qg, kg, vg, z, bg, gdn_norm[l][None, :], tc=tc, gsz=GDN_GROUP)
        ob = _attn(qm, km, vm, tq=tq)
        h = _merge(h, oa.reshape(n, d), ob.reshape(n, d), sa.reshape(n, d), sb.reshape(n, d),
                   proj_a[l].astype(BF), proj_b[l].astype(BF), w_o[l].astype(BF), tm=tm_ffn)
        h = _ffn(h, ffn2_norm[l][None, :], ffn2_w_gate[l].astype(BF), ffn2_w_up[l].astype(BF),
                 ffn2_w_down[l].astype(BF), final_norm[None, :] if last else ones,
                 final=last, tm=tm_ffn)
    return h.reshape(b, t, d)
```

```python
import functools

import numpy as np
import jax
import jax.numpy as jnp
from jax import lax
from jax.experimental import pallas as pl
from jax.experimental.pallas import tpu as pltpu

EPS = 1e-6
CONV_W = 4
GDN_HEADS = 8
GDN_DK = 128
GDN_DV = 128
GDN_CHUNK = 64
GDN_GROUP = 2
MLA_HEADS = 8
Q_LORA = 384
KV_LORA = 256
QK_NOPE = 128
QK_ROPE = 64
V_HEAD = 128
ROPE_THETA = 10000.0
LANES = 128
CONV_HALO = 8
assert CONV_HALO >= CONV_W - 1
VMEM_LIMIT_BYTES = 56 * 1024 * 1024
FFN_ROWS = 512
INPROJ_ROWS = 256
GDN_ROWS = 2048
ATTN_ROWS = 512

BF = jnp.bfloat16
F32 = jnp.float32
NEG_BIG = -1e30


def _dot(a, b):
    return jnp.dot(a, b, preferred_element_type=F32)


def _dot_nt(a, b):
    return lax.dot_general(a, b, (((1,), (1,)), ((), ())), preferred_element_type=F32)


def _dot_tn(a, b):
    return lax.dot_general(a, b, (((0,), (0,)), ((), ())), preferred_element_type=F32)


def _rms(x, w):
    return x * lax.rsqrt(jnp.mean(x * x, axis=-1, keepdims=True) + EPS) * w


def _sigmoid(x):
    return 0.5 * jnp.tanh(0.5 * x) + 0.5


def _silu(x):
    h = 0.5 * x
    return h * jnp.tanh(h) + h


def _resident(shape):
    nd = len(shape)
    return pl.BlockSpec(shape, lambda *_: (0,) * nd, pipeline_mode=pl.Buffered(1))


def _ffn_kernel(x_ref, nw_ref, wg_ref, wu_ref, wd_ref, fw_ref, o_ref, *, final):
    x = x_ref[...]
    xn = _rms(x, nw_ref[...]).astype(BF)
    g = _dot(xn, wg_ref[...])
    u = _dot(xn, wu_ref[...])
    a = (_silu(g) * u).astype(BF)
    h = x + 0.5 * _dot(a, wd_ref[...])
    if final:
        h = _rms(h, fw_ref[...])
    o_ref[...] = h


def _ffn(x, nw, wg, wu, wd, fw, *, final, tm):
    n, d = x.shape
    ff = wg.shape[1]
    row = pl.BlockSpec((tm, d), lambda i: (i, 0))
    return pl.pallas_call(
        functools.partial(_ffn_kernel, final=final),
        grid=(n // tm,),
        in_specs=[row, _resident((1, d)), _resident((d, ff)), _resident((d, ff)),
                  _resident((ff, d)), _resident((1, d))],
        out_specs=row,
        out_shape=jax.ShapeDtypeStruct((n, d), F32),
        compiler_params=pltpu.CompilerParams(
            dimension_semantics=("arbitrary",), vmem_limit_bytes=VMEM_LIMIT_BYTES),
        name="ffn_final" if final else "ffn",
    )(x, nw, wg, wu, wd, fw)


def _inproj_kernel(pos_ref, h_ref, nw_ref, wqkv_ref, convw_ref, wz_ref, wbg_ref, gp_ref,
                   wqd_ref, wckv_ref, wkpe_ref, wga_ref, wgb_ref, qan_ref, wqn_ref, wqp_ref,
                   wqpr_ref, kvan_ref, wkn_ref, wv_ref, invf_ref,
                   qg_ref, kg_ref, vg_ref, z_ref, bg_ref, qm_ref, km_ref, vm_ref, sa_ref, sb_ref,
                   xbuf_ref, *, tm):
    u = _rms(h_ref[0], nw_ref[...]).astype(BF)

    @pl.when(pl.program_id(1) == 0)
    def _():
        xbuf_ref[0:CONV_HALO, :] = jnp.zeros((CONV_HALO, xbuf_ref.shape[1]), F32)

    xbuf_ref[CONV_HALO:CONV_HALO + tm, :] = _dot(u, wqkv_ref[...])
    n_slab = xbuf_ref.shape[1] // LANES
    per = n_slab // 3
    for s in range(n_slab):
        cols = slice(LANES * s, LANES * (s + 1))
        xs = xbuf_ref[:, cols]
        acc = convw_ref[CONV_W - 1:CONV_W, cols] * xs
        for j in range(CONV_W - 1):
            acc = acc + pltpu.roll(convw_ref[j:j + 1, cols] * xs, CONV_W - 1 - j, 0)
        y = _silu(acc[CONV_HALO:, :])
        if s < 2 * per:
            inv = lax.rsqrt(jnp.sum(y * y, axis=-1, keepdims=True) + EPS)
            y = y * (inv * (GDN_DK ** -0.5) if s < per else inv)
        dst = (qg_ref, kg_ref, vg_ref)[s // per]
        oc = slice(LANES * (s % per), LANES * (s % per + 1))
        dst[0, :, oc] = y.astype(BF)
    xbuf_ref[0:CONV_HALO, :] = xbuf_ref[tm:tm + CONV_HALO, :]

    zz = _dot(u, wz_ref[...])
    z_ref[0] = _silu(zz).astype(BF)

    ba = _dot(u, wbg_ref[...])
    sp_in = ba + gp_ref[1:2, :]
    softplus = jnp.maximum(sp_in, 0.0) + jnp.log1p(jnp.exp(-jnp.abs(sp_in)))
    lane = lax.broadcasted_iota(jnp.int32, ba.shape, 1)
    bg_ref[0] = jnp.where(lane < GDN_HEADS, _sigmoid(ba), -jnp.exp(gp_ref[0:1, :]) * softplus)

    sa_ref[0] = _sigmoid(_dot(u, wga_ref[...])).astype(BF)
    sb_ref[0] = _sigmoid(_dot(u, wgb_ref[...])).astype(BF)

    qn = _rms(_dot(u, wqd_ref[...]), qan_ref[...]).astype(BF)
    cn = _rms(_dot(u, wckv_ref[...]), kvan_ref[...]).astype(BF)
    pos_rows = jnp.broadcast_to(pos_ref[0].astype(F32), (LANES, tm))
    ang = pos_rows.T * invf_ref[...]
    cos = jnp.cos(ang)
    sin = jnp.sin(ang)
    reps = (MLA_HEADS * QK_ROPE) // LANES
    cos_h = jnp.concatenate([cos] * reps, axis=-1)
    sin_h = jnp.concatenate([sin] * reps, axis=-1)
    scale = (QK_NOPE + QK_ROPE) ** -0.5 * float(np.log2(np.e))
    q_nope = _dot(qn, wqn_ref[...]) * scale
    q_pe = (_dot(qn, wqp_ref[...]) * cos_h + _dot(qn, wqpr_ref[...]) * sin_h) * scale
    kp = _dot(u, wkpe_ref[...])
    k_pe = (kp[:, :QK_ROPE] * cos[:, :QK_ROPE] + kp[:, QK_ROPE:] * sin[:, :QK_ROPE]).astype(BF)
    k_nope = _dot(cn, wkn_ref[...])
    v = _dot(cn, wv_ref[...])
    for hh in range(MLA_HEADS):
        qm_ref[0, hh, :, 0:QK_NOPE] = q_nope[:, QK_NOPE * hh:QK_NOPE * (hh + 1)].astype(BF)
        qm_ref[0, hh, :, QK_NOPE:] = q_pe[:, QK_ROPE * hh:QK_ROPE * (hh + 1)].astype(BF)
        km_ref[0, hh, :, 0:QK_NOPE] = k_nope[:, QK_NOPE * hh:QK_NOPE * (hh + 1)].astype(BF)
        km_ref[0, hh, :, QK_NOPE:] = k_pe
        vm_ref[0, hh] = v[:, V_HEAD * hh:V_HEAD * (hh + 1)].astype(BF)


def _inproj(pos, h, weights, *, tm):
    b, t, d = h.shape
    nt = t // tm
    conv_ch = weights[1].shape[1]
    qk_w = GDN_HEADS * GDN_DK
    v_w = GDN_HEADS * GDN_DV
    hq = QK_NOPE + QK_ROPE

    def row(width):
        return pl.BlockSpec((1, tm, width), lambda bi, ti: (bi, ti, 0))

    def head(width):
        return pl.BlockSpec((1, MLA_HEADS, tm, width), lambda bi, ti: (bi, 0, ti, 0))

    out_shape = (
        jax.ShapeDtypeStruct((b, t, qk_w), BF), jax.ShapeDtypeStruct((b, t, qk_w), BF),
        jax.ShapeDtypeStruct((b, t, v_w), BF), jax.ShapeDtypeStruct((b, t, v_w), BF),
        jax.ShapeDtypeStruct((b, t, 2 * GDN_HEADS), F32),
        jax.ShapeDtypeStruct((b, MLA_HEADS, t, hq), BF), jax.ShapeDtypeStruct((b, MLA_HEADS, t, hq), BF),
        jax.ShapeDtypeStruct((b, MLA_HEADS, t, V_HEAD), BF),
        jax.ShapeDtypeStruct((b, t, d), BF), jax.ShapeDtypeStruct((b, t, d), BF),
    )
    out_specs = (row(qk_w), row(qk_w), row(v_w), row(v_w), row(2 * GDN_HEADS),
                 head(hq), head(hq), head(V_HEAD), row(d), row(d))
    return pl.pallas_call(
        functools.partial(_inproj_kernel, tm=tm),
        grid=(b, nt),
        in_specs=[pl.BlockSpec((1, 1, tm), lambda bi, ti: (bi, 0, ti)), row(d)]
        + [_resident(w.shape) for w in weights],
        out_specs=out_specs,
        out_shape=out_shape,
        scratch_shapes=[pltpu.VMEM((tm + CONV_HALO, conv_ch), F32)],
        compiler_params=pltpu.CompilerParams(
            dimension_semantics=("arbitrary", "arbitrary"), vmem_limit_bytes=VMEM_LIMIT_BYTES),
        name="inproj",
    )(pos, h, *weights)


def _gdn_kernel(q_ref, k_ref, v_ref, z_ref, bg_ref, nw_ref, o_ref, state_ref, *, tc, gsz):
    c = GDN_CHUNK

    @pl.when(pl.program_id(1) == 0)
    def _():
        state_ref[...] = jnp.zeros(state_ref.shape, F32)

    ii = lax.broadcasted_iota(jnp.int32, (c, c), 0)
    jj = lax.broadcasted_iota(jnp.int32, (c, c), 1)
    eye = (ii == jj).astype(F32)
    heads = range(GDN_HEADS)
    nchunk = tc // c

    def local(chunks, out):
        probs = [(ci, hh) for ci in chunks for hh in heads]
        rows = [slice(ci * c, (ci + 1) * c) for ci, _ in probs]
        cols = [slice(LANES * hh, LANES * (hh + 1)) for _, hh in probs]
        ids = range(len(probs))
        bgs = {ci: bg_ref[0, ci * c:(ci + 1) * c, :] for ci in chunks}
        q = [q_ref[0, rows[i], cols[i]] for i in ids]
        k = [k_ref[0, rows[i], cols[i]] for i in ids]
        v = [v_ref[0, rows[i], cols[i]] for i in ids]
        beta = [bgs[ci][:, hh:hh + 1] for ci, hh in probs]
        g = [bgs[ci][:, GDN_HEADS + hh:GDN_HEADS + hh + 1] for ci, hh in probs]
        gc_row = [jnp.sum(jnp.where(ii <= jj, x, 0.0), axis=0, keepdims=True) for x in g]
        gc_col = [jnp.sum(eye * x, axis=1, keepdims=True) for x in gc_row]
        g_last = [jnp.sum(x, axis=0, keepdims=True) for x in g]
        decay = [jnp.where(ii >= jj, jnp.exp(jnp.minimum(gc - gr, 0.0)), 0.0)
                 for gc, gr in zip(gc_col, gc_row)]
        e_col = [jnp.exp(x) for x in gc_col]
        kf = [x.astype(F32) for x in k]
        kb = [x * b for x, b in zip(kf, beta)]
        yield
        both = [_dot_nt(jnp.concatenate([kb[i].astype(BF), q[i]], axis=0), k[i]) for i in ids]
        n = [jnp.where(ii > jj, -both[i][:c] * decay[i], 0.0) for i in ids]
        attn = [(both[i][c:] * decay[i]).astype(BF) for i in ids]
        yield
        a = [eye + x for x in n]
        nb = [x.astype(BF) for x in n]
        p = [_dot(x, x) for x in nb]
        yield
        for _ in range(4):
            pb = [x.astype(BF) for x in p]
            sq = [_dot(jnp.concatenate([pb[i], a[i].astype(BF)], axis=0), pb[i]) for i in ids]
            p = [x[:c] for x in sq]
            a = [a[i] + sq[i][c:] for i in ids]
            yield
        a = [a[i] + _dot(a[i].astype(BF), p[i].astype(BF)) for i in ids]
        rhs = [jnp.concatenate([v[i].astype(F32) * beta[i], kb[i] * e_col[i]], axis=1).astype(BF)
               for i in ids]
        yield
        sol = [_dot(a[i].astype(BF), rhs[i]) for i in ids]
        for i, (ci, hh) in enumerate(probs):
            out[ci, hh] = dict(
                rows=rows[i], cols=cols[i], u=sol[i][:, :GDN_DV], attn=attn[i],
                wq=jnp.concatenate([sol[i][:, GDN_DV:].astype(BF),
                                    (q[i].astype(F32) * e_col[i]).astype(BF)], axis=0),
                k_dec=(kf[i] * jnp.exp(g_last[i] - gc_col[i])).astype(BF),
                e_last=jnp.exp(g_last[i]))
        yield

    def recurrence(chunks, loc, state):
        for ci in chunks:
            pr = [loc[ci, hh] for hh in heads]
            ws = [_dot(pr[hh]["wq"], state[hh].astype(BF)) for hh in heads]
            v_new = [(pr[hh]["u"] - ws[hh][:c]).astype(BF) for hh in heads]
            yield
            o = [ws[hh][c:] + _dot(pr[hh]["attn"], v_new[hh]) for hh in heads]
            for hh in heads:
                state[hh] = state[hh] * pr[hh]["e_last"] + _dot_tn(pr[hh]["k_dec"], v_new[hh])
            for hh in heads:
                on = _rms(o[hh], nw_ref[...])
                rows, cols = pr[hh]["rows"], pr[hh]["cols"]
                o_ref[0, rows, cols] = (on * z_ref[0, rows, cols].astype(F32)).astype(BF)
            yield

    state = [state_ref[hh] for hh in heads]
    groups = [list(range(gi, min(gi + gsz, nchunk))) for gi in range(0, nchunk, gsz)]
    loc = {}
    rec = iter(())
    for grp in groups:
        for _ in local(grp, loc):
            next(rec, None)
        for _ in rec:
            pass
        rec = recurrence(grp, loc, state)
    for _ in rec:
        pass
    for hh in heads:
        state_ref[hh] = state[hh]


def _gdn(qg, kg, vg, z, bg, nw, *, tc, gsz):
    b, t, w = qg.shape
    row = pl.BlockSpec((1, tc, w), lambda bi, ti: (bi, ti, 0))
    return pl.pallas_call(
        functools.partial(_gdn_kernel, tc=tc, gsz=gsz),
        grid=(b, t // tc),
        in_specs=[row, row, row, row,
                  pl.BlockSpec((1, tc, bg.shape[2]), lambda bi, ti: (bi, ti, 0)),
                  _resident(nw.shape)],
        out_specs=row,
        out_shape=jax.ShapeDtypeStruct((b, t, w), BF),
        scratch_shapes=[pltpu.VMEM((GDN_HEADS, GDN_DK, GDN_DV), F32)],
        compiler_params=pltpu.CompilerParams(dimension_semantics=("arbitrary", "arbitrary")),
        name="gdn",
    )(qg, kg, vg, z, bg, nw)


def _attn_kernel(q_ref, k_ref, v_ref, o_ref, sa_ref, sb_ref, ma_ref, mb_ref, m_ref, l_ref, acc_ref,
                 *, tq):
    t = q_ref.shape[2]
    reps = tq // LANES
    ii = lax.broadcasted_iota(jnp.int32, (tq, tq), 0)
    jj = lax.broadcasted_iota(jnp.int32, (tq, tq), 1)
    bufs = ((sa_ref, ma_ref), (sb_ref, mb_ref))

    def rows_of(blk):
        return slice(blk * tq, (blk + 1) * tq)

    def scores(step, buf):
        qi, kb = step
        s_ref, mx_ref = buf
        s = _dot_nt(q_ref[0, 0, rows_of(qi), :], k_ref[0, 0, rows_of(kb), :])
        if kb == qi:
            s = jnp.where(jj <= ii, s, NEG_BIG)
        s_ref[...] = s
        mx_ref[...] = jnp.broadcast_to(jnp.max(s, axis=-1, keepdims=True), mx_ref.shape)

    def update(step, buf):
        qi, kb = step
        s_ref, mx_ref = buf
        first = kb == qi
        last = kb == qi - 1 or qi == 0
        if first:
            m_next = mx_ref[...]
        else:
            m_prev = m_ref[...]
            m_next = jnp.maximum(m_prev, mx_ref[...])
            alpha = jnp.exp2(m_prev - m_next)
        m_ref[...] = m_next
        p = jnp.exp2(s_ref[...] - jnp.concatenate([m_next] * reps, axis=-1))
        part = p[:, 0:LANES]
        for w in range(1, reps):
            part = part + p[:, w * LANES:(w + 1) * LANES]
        pv = _dot(p.astype(BF), v_ref[0, 0, rows_of(kb), :])
        if first:
            l_ref[...] = part
            acc_ref[...] = pv
        else:
            l_ref[...] = alpha * l_ref[...] + part
            acc_ref[...] = alpha * acc_ref[...] + pv
        if last:
            l_tot = jnp.sum(l_ref[...], axis=-1, keepdims=True)
            o_ref[0, rows_of(qi), :] = (acc_ref[...] / l_tot).astype(BF)

    steps = [(qi, kb) for qi in range(t // tq) for kb in [qi] + list(range(qi))]
    scores(steps[0], bufs[0])
    for si, step in enumerate(steps):
        if si + 1 < len(steps):
            scores(steps[si + 1], bufs[(si + 1) % 2])
        update(step, bufs[si % 2])


def _attn(qm, km, vm, *, tq):
    b, h, t, hq = qm.shape
    dv = vm.shape[3]
    assert dv == LANES
    return pl.pallas_call(
        functools.partial(_attn_kernel, tq=tq),
        grid=(b, h),
        in_specs=[pl.BlockSpec((1, 1, t, hq), lambda bi, hi: (bi, hi, 0, 0)),
                  pl.BlockSpec((1, 1, t, hq), lambda bi, hi: (bi, hi, 0, 0)),
                  pl.BlockSpec((1, 1, t, dv), lambda bi, hi: (bi, hi, 0, 0))],
        out_specs=pl.BlockSpec((1, t, dv), lambda bi, hi: (bi, 0, hi)),
        out_shape=jax.ShapeDtypeStruct((b, t, h * dv), BF),
        scratch_shapes=[pltpu.VMEM((tq, tq), F32), pltpu.VMEM((tq, tq), F32),
                        pltpu.VMEM((tq, LANES), F32), pltpu.VMEM((tq, LANES), F32),
                        pltpu.VMEM((tq, LANES), F32), pltpu.VMEM((tq, LANES), F32),
                        pltpu.VMEM((tq, dv), F32)],
        compiler_params=pltpu.CompilerParams(dimension_semantics=("arbitrary", "arbitrary")),
        name="attn",
    )(qm, km, vm)


def _merge_kernel(h_ref, oa_ref, ob_ref, sa_ref, sb_ref, pa_ref, pb_ref, wo_ref, o_ref):
    ya = _dot(oa_ref[...], pa_ref[...])
    yb = _dot(ob_ref[...], pb_ref[...])
    merged = sa_ref[...].astype(F32) * ya + sb_ref[...].astype(F32) * yb
    o_ref[...] = h_ref[...] + _dot(merged.astype(BF), wo_ref[...])


def _merge(h, oa, ob, sa, sb, pa, pb, wo, *, tm):
    n, d = h.shape
    row = pl.BlockSpec((tm, d), lambda i: (i, 0))
    return pl.pallas_call(
        _merge_kernel,
        grid=(n // tm,),
        in_specs=[row, row, row, row, row, _resident(pa.shape), _resident(pb.shape),
                  _resident(wo.shape)],
        out_specs=row,
        out_shape=jax.ShapeDtypeStruct((n, d), F32),
        compiler_params=pltpu.CompilerParams(dimension_semantics=("arbitrary",)),
        name="merge",
    )(h, oa, ob, sa, sb, pa, pb, wo)


def _rot_cols(w):
    k, n = w.shape
    w4 = w.reshape(k, n // QK_ROPE, 2, QK_ROPE // 2)
    return jnp.concatenate([-w4[:, :, 1], w4[:, :, 0]], axis=-1).reshape(k, n)


def _mixer_weights(mix_norm, w_in, conv_w, a_log, dt_bias, q_a_norm, w_q_up, kv_a_norm, w_kv_up):
    d = w_in.shape[0]
    qk_w = GDN_HEADS * GDN_DK
    v_w = GDN_HEADS * GDN_DV
    widths = (qk_w, qk_w, v_w, v_w, GDN_HEADS, GDN_HEADS, Q_LORA, KV_LORA, QK_ROPE, d, d)
    offs = np.concatenate([[0], np.cumsum(widths)])
    w_in = w_in.astype(BF)
    seg = [w_in[:, int(offs[i]):int(offs[i + 1])] for i in range(len(widths))]
    wqkv = jnp.concatenate(seg[0:3], axis=1)
    wz = seg[3]
    wbg = jnp.concatenate(seg[4:6], axis=1)
    wqd = seg[6]
    wckv = seg[7]
    wkpe = jnp.concatenate([seg[8], _rot_cols(seg[8])], axis=1)
    wga = seg[9]
    wgb = seg[10]
    zeros = jnp.zeros((GDN_HEADS,), F32)
    gp = jnp.stack([jnp.concatenate([zeros, a_log]), jnp.concatenate([zeros, dt_bias])])
    wq = w_q_up.reshape(Q_LORA, MLA_HEADS, QK_NOPE + QK_ROPE)
    wqn = wq[:, :, :QK_NOPE].reshape(Q_LORA, MLA_HEADS * QK_NOPE).astype(BF)
    wqp_f = wq[:, :, QK_NOPE:].reshape(Q_LORA, MLA_HEADS * QK_ROPE)
    wqp = wqp_f.astype(BF)
    wqpr = _rot_cols(wqp_f).astype(BF)
    wkv = w_kv_up.reshape(KV_LORA, MLA_HEADS, QK_NOPE + V_HEAD)
    wkn = wkv[:, :, :QK_NOPE].reshape(KV_LORA, MLA_HEADS * QK_NOPE).astype(BF)
    wv = wkv[:, :, QK_NOPE:].reshape(KV_LORA, MLA_HEADS * V_HEAD).astype(BF)
    inv_freq = ROPE_THETA ** (-jnp.arange(0, QK_ROPE, 2, dtype=F32) / QK_ROPE)
    invf = jnp.tile(inv_freq, LANES // (QK_ROPE // 2))[None, :]
    return (mix_norm[None, :], wqkv, conv_w, wz, wbg, gp, wqd, wckv, wkpe, wga, wgb,
            q_a_norm[None, :], wqn, wqp, wqpr, kv_a_norm[None, :], wkn, wv, invf)


def kernel(x, positions, ffn1_norm, ffn1_w_gate, ffn1_w_up, ffn1_w_down, mix_norm, w_in, conv_w, a_log, dt_bias, gdn_norm, proj_a, q_a_norm, w_q_up, kv_a_norm, w_kv_up, proj_b, w_o, ffn2_norm, ffn2_w_gate, ffn2_w_up, ffn2_w_down, final_norm):
    b, t, d = x.shape
    n = b * t
    depth = ffn1_norm.shape[0]
    tm_ffn = min(FFN_ROWS, n)
    tm_in = min(INPROJ_ROWS, t)
    tc = min(GDN_ROWS, t)
    tq = min(ATTN_ROWS, t)
    assert n % tm_ffn == 0 and t % tm_in == 0 and t % tc == 0 and t % tq == 0
    assert tc % (GDN_CHUNK * GDN_GROUP) == 0 and tq % LANES == 0
    ones = jnp.ones((1, d), F32)
    pos = positions.reshape(b, 1, t)
    h = x.reshape(n, d)
    for l in range(depth):
        last = l == depth - 1
        h = _ffn(h, ffn1_norm[l][None, :], ffn1_w_gate[l].astype(BF), ffn1_w_up[l].astype(BF),
                 ffn1_w_down[l].astype(BF), ones, final=False, tm=tm_ffn)
        mw = _mixer_weights(mix_norm[l], w_in[l], conv_w[l], a_log[l], dt_bias[l], q_a_norm[l],
                            w_q_up[l], kv_a_norm[l], w_kv_up[l])
        qg, kg, vg, z, bg, qm, km, vm, sa, sb = _inproj(pos, h.reshape(b, t, d), mw, tm=tm_in)
        oa = _gdn(qg, kg, vg, z, bg, gdn_norm[l][None, :], tc=tc, gsz=GDN_GROUP)
        ob = _attn(qm, km, vm, tq=tq)
        h = _merge(h, oa.reshape(n, d), ob.reshape(n, d), sa.reshape(n, d), sb.reshape(n, d),
                   proj_a[l].astype(BF), proj_b[l].astype(BF), w_o[l].astype(BF), tm=tm_ffn)
        h = _ffn(h, ffn2_norm[l][None, :], ffn2_w_gate[l].astype(BF), ffn2_w_up[l].astype(BF),
                 ffn2_w_down[l].astype(BF), final_norm[None, :] if last else ones,
                 final=last, tm=tm_ffn)
    return h.reshape(b, t, d)
```
